```python
import math
import jax, jax.numpy as jnp
from jax import lax
import numpy as np

D_MODEL = 1024
BATCH = 8
SEQ = 2048
DEPTH = 4

HEAD_DIM = 64
MIX_WIDTH = D_MODEL
DIFF_WIDTH = MIX_WIDTH // 2
MOBA_WIDTH = MIX_WIDTH - DIFF_WIDTH
DIFF_HEADS = DIFF_WIDTH // (2 * HEAD_DIM)
DIFF_V_DIM = 2 * HEAD_DIM
MOBA_HEADS = MOBA_WIDTH // HEAD_DIM
PROJ_WIDTH = 3 * DIFF_WIDTH + 3 * MOBA_WIDTH
N_BIAS_HEADS = DIFF_HEADS + MOBA_HEADS
D_FF = ((8 * D_MODEL // 3 + 127) // 128) * 128
CONV_WIDTH = 3
DIFF_Q_BLOCK = 128
MOBA_BLOCK = 256
MOBA_TOPK = 3
MOBA_Q_CHUNK = 64
REL_BUCKETS = 32
REL_MAX_EXACT = REL_BUCKETS // 2
REL_MAX_DISTANCE = 1024
NORM_EPS = 1e-6

kernel_name = "hymba_diffattn_moba_convffn_trunk"


def rms_norm(x, g):
    xf = x.astype(jnp.float32)
    y = xf * lax.rsqrt(jnp.mean(xf * xf, axis=-1, keepdims=True) + NORM_EPS)
    return (y * g).astype(x.dtype)


def rel_bucket(dist):
    n = jnp.maximum(dist, 0)
    nf = jnp.maximum(n, REL_MAX_EXACT).astype(jnp.float32)
    large = REL_MAX_EXACT + (jnp.log(nf / REL_MAX_EXACT) / math.log(REL_MAX_DISTANCE / REL_MAX_EXACT)
                             * (REL_BUCKETS - REL_MAX_EXACT)).astype(jnp.int32)
    large = jnp.minimum(large, REL_BUCKETS - 1)
    return jnp.where(n < REL_MAX_EXACT, n, large)


def diff_attention(q, k, v, lam, lam_init, sub_g, bias_tab):
    B, S, H = q.shape[:3]
    nqb = S // DIFF_Q_BLOCK
    qb = q.reshape(B, nqb, DIFF_Q_BLOCK, H, 2, HEAD_DIM).transpose(1, 0, 2, 3, 4, 5)
    kpos = jnp.arange(S)
    scale = HEAD_DIM ** -0.5

    def block(args):
        qblk, i = args
        qpos = i * DIFF_Q_BLOCK + jnp.arange(DIFF_Q_BLOCK)
        dist = qpos[:, None] - kpos[None, :]
        bias = jnp.moveaxis(bias_tab[rel_bucket(dist)], -1, 0)
        logits = jnp.einsum('bqhmd,bkhmd->bmhqk', qblk, k).astype(jnp.float32) * scale + bias
        logits = jnp.where(dist >= 0, logits, -jnp.inf)
        p = jax.nn.softmax(logits, axis=-1)
        a = p[:, 0] - lam * p[:, 1]
        return jnp.einsum('bhqk,bkhe->bqhe', a.astype(v.dtype), v)

    out = lax.map(block, (qb, jnp.arange(nqb)))
    out = out.transpose(1, 0, 2, 3, 4).reshape(B, S, H, DIFF_V_DIM)
    out = rms_norm(out, sub_g) * (1.0 - lam_init)
    return out.reshape(B, S, H * DIFF_V_DIM)


def moba_attention(q, k, v, bias_tab):
    B, S, H, D = q.shape
    nb = -(-S // MOBA_BLOCK)
    pad = nb * MOBA_BLOCK - S
    padw = ((0, 0), (0, pad), (0, 0), (0, 0))
    kb = jnp.pad(k, padw).reshape(B, nb, MOBA_BLOCK, H, D).transpose(0, 3, 1, 2, 4)
    vb = jnp.pad(v, padw).reshape(B, nb, MOBA_BLOCK, H, D).transpose(0, 3, 1, 2, 4)
    kmean = jnp.mean(kb.astype(jnp.float32), axis=3)
    gate = jnp.einsum('bshd,bhnd->bhsn', q.astype(jnp.float32), kmean)
    past = jnp.arange(nb)[None, :] < (jnp.arange(S) // MOBA_BLOCK)[:, None]
    gate = jnp.where(past, gate, -jnp.inf)
    topk = min(MOBA_TOPK, nb)
    sel_score, sel_idx = lax.top_k(gate, topk)
    sel_valid = jnp.isfinite(sel_score)

    nc = S // MOBA_Q_CHUNK
    qc = q.reshape(B, nc, MOBA_Q_CHUNK, H, D).transpose(1, 0, 3, 2, 4)
    idx_c = sel_idx.reshape(B, H, nc, MOBA_Q_CHUNK, topk).transpose(2, 0, 1, 3, 4)
    val_c = sel_valid.reshape(B, H, nc, MOBA_Q_CHUNK, topk).transpose(2, 0, 1, 3, 4)
    bi = jnp.arange(B)[:, None, None, None]
    hi = jnp.arange(H)[None, :, None, None]
    hi5 = jnp.arange(H)[None, :, None, None, None]
    tab_h = bias_tab.T
    j = jnp.arange(MOBA_BLOCK)
    scale = D ** -0.5
    n_sel = topk * MOBA_BLOCK

    def chunk(args):
        qch, idx, valid, c = args
        qpos = c * MOBA_Q_CHUNK + jnp.arange(MOBA_Q_CHUNK)
        gk = kb[bi, hi, idx]
        gv = vb[bi, hi, idx]
        dist_p = qpos[None, None, :, None, None] - (idx[..., None] * MOBA_BLOCK + j)
        bias_p = tab_h[hi5, rel_bucket(dist_p)]
        lp = jnp.einsum('bhqd,bhqnjd->bhqnj', qch, gk).astype(jnp.float32) * scale + bias_p
        lp = jnp.where(valid[..., None], lp, -jnp.inf).reshape(B, H, MOBA_Q_CHUNK, n_sel)
        ob = (c * MOBA_Q_CHUNK) // MOBA_BLOCK
        ko = lax.dynamic_index_in_dim(kb, ob, axis=2, keepdims=False)
        vo = lax.dynamic_index_in_dim(vb, ob, axis=2, keepdims=False)
        dist_o = qpos[:, None] - (ob * MOBA_BLOCK + j)[None, :]
        bias_o = tab_h[:, rel_bucket(dist_o)]
        lo = jnp.einsum('bhqd,bhjd->bhqj', qch, ko).astype(jnp.float32) * scale + bias_o
        lo = jnp.where(dist_o >= 0, lo, -jnp.inf)
        p = jax.nn.softmax(jnp.concatenate([lp, lo], axis=-1), axis=-1).astype(v.dtype)
        pp = p[..., :n_sel].reshape(B, H, MOBA_Q_CHUNK, topk, MOBA_BLOCK)
        po = p[..., n_sel:]
        return (jnp.einsum('bhqnj,bhqnje->bhqe', pp, gv)
                + jnp.einsum('bhqj,bhje->bhqe', po, vo))

    out = lax.map(chunk, (qc, idx_c, val_c, jnp.arange(nc)))
    return out.transpose(1, 0, 3, 2, 4).reshape(B, S, H * D)


def causal_dwconv(u, w, b):
    C = u.shape[-1]
    y = lax.conv_general_dilated(u, w[:, None, :], window_strides=(1,),
                                 padding=[(CONV_WIDTH - 1, 0)],
                                 dimension_numbers=('NWC', 'WIO', 'NWC'),
                                 feature_group_count=C)
    return y + b


def conv_ffn(h, w_up, conv_w, conv_b, w_down):
    u = causal_dwconv(h @ w_up, conv_w, conv_b)
    g, up = jnp.split(u, 2, axis=-1)
    return (jax.nn.silu(g) * up) @ w_down


def setup_inputs(seed: int = 0) -> dict:
    key = jax.random.key(seed)
    ks = jax.random.split(key, 14)
    f32 = jnp.float32
    nrm = lambda k, s: jax.random.normal(k, s, f32)
    resid = (2 * DEPTH) ** -0.5
    return {
        'x': nrm(ks[0], (BATCH, SEQ, D_MODEL)),
        'ln_attn_g': 1.0 + 0.02 * nrm(ks[1], (DEPTH, D_MODEL)),
        'w_in': nrm(ks[2], (DEPTH, D_MODEL, PROJ_WIDTH)) * D_MODEL ** -0.5,
        'qk_norm_g': 1.0 + 0.02 * nrm(ks[3], (DEPTH, 4, HEAD_DIM)),
        'diff_lambda': 0.1 * nrm(ks[4], (DEPTH, 4, HEAD_DIM)),
        'diff_subln_g': 1.0 + 0.02 * nrm(ks[5], (DEPTH, DIFF_V_DIM)),
        'w_out': nrm(ks[6], (DEPTH, MIX_WIDTH, D_MODEL)) * MIX_WIDTH ** -0.5 * resid,
        'ln_ffn_g': 1.0 + 0.02 * nrm(ks[7], (DEPTH, D_MODEL)),
        'w_up': nrm(ks[8], (DEPTH, D_MODEL, 2 * D_FF)) * D_MODEL ** -0.5,
        'conv_w': nrm(ks[9], (DEPTH, CONV_WIDTH, 2 * D_FF)) * CONV_WIDTH ** -0.5,
        'conv_b': 0.02 * nrm(ks[10], (DEPTH, 2 * D_FF)),
        'w_down': nrm(ks[11], (DEPTH, D_FF, D_MODEL)) * D_FF ** -0.5 * resid,
        'rel_bias': 0.5 * nrm(ks[12], (REL_BUCKETS, N_BIAS_HEADS)),
    }


def reference(x, ln_attn_g, w_in, qk_norm_g, diff_lambda, diff_subln_g, w_out,
              ln_ffn_g, w_up, conv_w, conv_b, w_down, rel_bias):
    B, S = x.shape[:2]
    cuts = [DIFF_WIDTH, 2 * DIFF_WIDTH, 3 * DIFF_WIDTH,
            3 * DIFF_WIDTH + MOBA_WIDTH, 3 * DIFF_WIDTH + 2 * MOBA_WIDTH]
    bias_diff = rel_bias[:, :DIFF_HEADS]
    bias_moba = rel_bias[:, DIFF_HEADS:]
    for i in range(DEPTH):
        lam_init = 0.8 - 0.6 * math.exp(-0.3 * i)
        h = rms_norm(x, ln_attn_g[i])
        dq, dk, dv, mq, mk, mv = jnp.split(h @ w_in[i], cuts, axis=-1)
        dq = rms_norm(dq.reshape(B, S, DIFF_HEADS, 2, HEAD_DIM), qk_norm_g[i, 0])
        dk = rms_norm(dk.reshape(B, S, DIFF_HEADS, 2, HEAD_DIM), qk_norm_g[i, 1])
        dv = dv.reshape(B, S, DIFF_HEADS, DIFF_V_DIM)
        lf = diff_lambda[i].astype(jnp.float32)
        lam = jnp.exp(jnp.sum(lf[0] * lf[1])) - jnp.exp(jnp.sum(lf[2] * lf[3])) + lam_init
        y_diff = diff_attention(dq, dk, dv, lam, lam_init, diff_subln_g[i], bias_diff)
        mq = rms_norm(mq.reshape(B, S, MOBA_HEADS, HEAD_DIM), qk_norm_g[i, 2])
        mk = rms_norm(mk.reshape(B, S, MOBA_HEADS, HEAD_DIM), qk_norm_g[i, 3])
        mv = mv.reshape(B, S, MOBA_HEADS, HEAD_DIM)
        y_moba = moba_attention(mq, mk, mv, bias_moba)
        x = x + jnp.concatenate([y_diff, y_moba], axis=-1) @ w_out[i]
        x = x + conv_ffn(rms_norm(x, ln_ffn_g[i]), w_up[i], conv_w[i], conv_b[i], w_down[i])
    return x
```

```python
import functools
import math

import jax
import jax.numpy as jnp
from jax import lax
from jax.experimental import pallas as pl
from jax.experimental.pallas import tpu as pltpu

D_MODEL = 1024
HEAD_DIM = 64
LANES = 128
N_GROUPS = 6
GROUP_WIDTH = 512
N_COLBLK = N_GROUPS * GROUP_WIDTH // LANES
BLK_PER_GROUP = GROUP_WIDTH // LANES
D_FF = 2816
CONV_WIDTH = 3
MOBA_BLOCK = 256
MOBA_TOPK = 3
REL_BUCKETS = 32
REL_MAX_EXACT = 16
REL_MAX_DISTANCE = 1024
NORM_EPS = 1e-6
NEG = -1e30

ATT_TILE = 256
PROJ_ROWS = 512
FFN_ROWS = 512
FFN_CHUNK = 256
HALO = 8
VMEM_LIMIT = 56 * 1024 * 1024

_F32 = jnp.float32
_BF16 = jnp.bfloat16


def _dot(a, b):
    return jnp.dot(a, b, preferred_element_type=_F32)


def _dot_nt(a, b):
    return lax.dot_general(a, b, (((1,), (1,)), ((), ())), preferred_element_type=_F32)


def _proj_kernel(x_ref, g_ref, w_ref, qkg_ref, o_ref):
    x = x_ref[...]
    ms = jnp.mean(x * x, axis=-1, keepdims=True)
    h = (x * lax.rsqrt(ms + NORM_EPS) * g_ref[...]).astype(_BF16)
    rows = x.shape[0]
    low_half = lax.broadcasted_iota(jnp.int32, (rows, LANES), 1) < HEAD_DIM
    norm_groups = {0: (0, HEAD_DIM ** -0.5), 1: (1, 1.0), 3: (2, HEAD_DIM ** -0.5), 4: (3, 1.0)}
    for grp in range(N_GROUPS):
        yg = _dot(h, w_ref[:, grp * GROUP_WIDTH:(grp + 1) * GROUP_WIDTH])
        for c in range(BLK_PER_GROUP):
            y = yg[:, c * LANES:(c + 1) * LANES]
            if grp in norm_groups:
                gain_row, scale = norm_groups[grp]
                sq = y * y
                tot = jnp.sum(sq, axis=-1, keepdims=True)
                lo = jnp.sum(jnp.where(low_half, sq, 0.0), axis=-1, keepdims=True)
                ms_h = jnp.where(low_half, lo, tot - lo) * (1.0 / HEAD_DIM)
                gain = qkg_ref[gain_row:gain_row + 1, :]
                y = y * lax.rsqrt(ms_h + NORM_EPS) * gain
                if scale != 1.0:
                    y = y * scale
            o_ref[grp * BLK_PER_GROUP + c] = y.astype(_BF16)


def _proj(x2d, g, w_bf16, qkg):
    n = x2d.shape[0]
    return pl.pallas_call(
        _proj_kernel,
        out_shape=jax.ShapeDtypeStruct((N_COLBLK, n, LANES), _BF16),
        grid=(n // PROJ_ROWS,),
        in_specs=[
            pl.BlockSpec((PROJ_ROWS, D_MODEL), lambda r: (r, 0)),
            pl.BlockSpec((1, D_MODEL), lambda r: (0, 0)),
            pl.BlockSpec((D_MODEL, N_GROUPS * GROUP_WIDTH), lambda r: (0, 0),
                         pipeline_mode=pl.Buffered(1)),
            pl.BlockSpec((4, LANES), lambda r: (0, 0)),
        ],
        out_specs=pl.BlockSpec((N_COLBLK, PROJ_ROWS, LANES), lambda r: (0, r, 0)),
        compiler_params=pltpu.CompilerParams(
            dimension_semantics=("arbitrary",), vmem_limit_bytes=VMEM_LIMIT),
        name="proj",
    )(x2d, g, w_bf16, qkg)


def _init_state(m_ref, l_ref, acc_ref):
    m_ref[...] = jnp.full(m_ref.shape, NEG, _F32)
    l_ref[...] = jnp.zeros(l_ref.shape, _F32)
    acc_ref[...] = jnp.zeros(acc_ref.shape, _F32)


def _online_softmax_step(s, v_t, idx, m_ref, l_ref, acc_ref):
    m_prev = m_ref[idx]
    m_new = jnp.maximum(m_prev, jnp.max(s, axis=-1, keepdims=True))
    alpha = jnp.exp(m_prev - m_new)
    p = jnp.exp(s - m_new)
    l_ref[idx] = alpha * l_ref[idx] + jnp.sum(p, axis=-1, keepdims=True)
    acc_ref[idx] = alpha * acc_ref[idx] + _dot(p.astype(_BF16), v_t)
    m_ref[idx] = m_new


def _split_halves(q):
    low_half = lax.broadcasted_iota(jnp.int32, q.shape, 1) < HEAD_DIM
    zero = jnp.zeros_like(q)
    return jnp.where(low_half, q, zero), jnp.where(low_half, zero, q)


def _diff_kernel(lam_init_ref, lamp_ref, q_ref, k_ref, v_ref, bias_ref, subg_ref, o_ref,
                 m_ref, l_ref, acc_ref):
    i = pl.program_id(2)
    t = ATT_TILE
    q_lo, q_hi = _split_halves(q_ref[0])
    _init_state(m_ref, l_ref, acc_ref)

    def tile(j, delta):
        k_t = k_ref[0, pl.ds(pl.multiple_of(j * t, t), t), :]
        v_t = v_ref[0, pl.ds(pl.multiple_of(j * t, t), t), :]
        bias = bias_ref[0, delta]
        _online_softmax_step(_dot_nt(q_lo, k_t) + bias, v_t, 0, m_ref, l_ref, acc_ref)
        _online_softmax_step(_dot_nt(q_hi, k_t) + bias, v_t, 1, m_ref, l_ref, acc_ref)

    tile(i, 0)

    def body(j, carry):
        tile(j, i - j)
        return carry

    lax.fori_loop(0, i, body, 0)

    lf = lamp_ref[...]
    lam_init = lam_init_ref[0]
    lam = (jnp.exp(jnp.sum(lf[0:1, :] * lf[1:2, :], axis=-1, keepdims=True))
           - jnp.exp(jnp.sum(lf[2:3, :] * lf[3:4, :], axis=-1, keepdims=True)) + lam_init)
    out = acc_ref[0] / l_ref[0] - lam * (acc_ref[1] / l_ref[1])
    ms = jnp.mean(out * out, axis=-1, keepdims=True)
    out = out * lax.rsqrt(ms + NORM_EPS) * subg_ref[...] * (1.0 - lam_init)
    o_ref[0] = out.astype(_BF16)


def _diff_attention(qkv, bias_tiles, lam_init, lam_params, sub_g, batch, seq):
    nq = seq // ATT_TILE
    heads = BLK_PER_GROUP
    return pl.pallas_call(
        _diff_kernel,
        out_shape=jax.ShapeDtypeStruct((heads, batch * seq, LANES), _BF16),
        grid=(heads, batch, nq),
        in_specs=[
            pl.BlockSpec(memory_space=pltpu.SMEM),
            pl.BlockSpec((4, HEAD_DIM), lambda h, b, i: (0, 0)),
            pl.BlockSpec((1, ATT_TILE, LANES), lambda h, b, i: (h, b * nq + i, 0)),
            pl.BlockSpec((1, seq, LANES), lambda h, b, i: (BLK_PER_GROUP + h, b, 0)),
            pl.BlockSpec((1, seq, LANES), lambda h, b, i: (2 * BLK_PER_GROUP + h, b, 0)),
            pl.BlockSpec((1, nq, ATT_TILE, ATT_TILE), lambda h, b, i: (h, 0, 0, 0)),
            pl.BlockSpec((1, LANES), lambda h, b, i: (0, 0)),
        ],
        out_specs=pl.BlockSpec((1, ATT_TILE, LANES), lambda h, b, i: (h, b * nq + i, 0)),
        scratch_shapes=[
            pltpu.VMEM((2, ATT_TILE, 1), _F32),
            pltpu.VMEM((2, ATT_TILE, 1), _F32),
            pltpu.VMEM((2, ATT_TILE, LANES), _F32),
        ],
        compiler_params=pltpu.CompilerParams(
            dimension_semantics=("arbitrary", "arbitrary", "arbitrary"),
            vmem_limit_bytes=VMEM_LIMIT),
        name="diff_attn",
    )(lam_init, lam_params, qkv, qkv, qkv, bias_tiles, sub_g)


def _moba_kernel(q_ref, k_ref, v_ref, bias_ref, o_ref, m_ref, l_ref, acc_ref, kmean_ref):
    i = pl.program_id(2)
    t = ATT_TILE
    nb = k_ref.shape[1] // MOBA_BLOCK

    @pl.when(i == 0)
    def _():
        kmean_ref[...] = jnp.zeros(kmean_ref.shape, _F32)
        for n in range(nb):
            kb = k_ref[0, n * MOBA_BLOCK:(n + 1) * MOBA_BLOCK, :].astype(_F32)
            kmean_ref[n:n + 1, :] = jnp.sum(kb, axis=0, keepdims=True) * (1.0 / MOBA_BLOCK)

    q_lo, q_hi = _split_halves(q_ref[0])
    _init_state(m_ref, l_ref, acc_ref)

    kmean = kmean_ref[...]
    km_hi = kmean.astype(_BF16)
    km_lo = (kmean - km_hi.astype(_F32)).astype(_BF16)

    def select_bias(q_half):
        gate = _dot_nt(q_half, km_hi) + _dot_nt(q_half, km_lo)
        cols = [gate[:, n:n + 1] for n in range(nb)]
        out = []
        for n in range(nb):
            rank = jnp.zeros_like(cols[n])
            for m in range(nb):
                if m == n:
                    continue
                beats = (cols[m] >= cols[n]) if m < n else (cols[m] > cols[n])
                past = (m < i).astype(_F32)
                rank = rank + jnp.where(beats, past, 0.0)
            finite = jnp.abs(cols[n]) < jnp.inf
            keep = (rank < MOBA_TOPK) & finite
            out.append(jnp.where(keep, 0.0, NEG))
        return out

    sel_a = select_bias(q_lo)
    sel_b = select_bias(q_hi)

    def tile(j, bias_a, bias_b):
        k_t = k_ref[0, pl.ds(pl.multiple_of(j * t, t), t), :]
        v_t = v_ref[0, pl.ds(pl.multiple_of(j * t, t), t), :]
        _online_softmax_step(_dot_nt(q_lo, k_t) + bias_a, v_t, 0, m_ref, l_ref, acc_ref)
        _online_softmax_step(_dot_nt(q_hi, k_t) + bias_b, v_t, 1, m_ref, l_ref, acc_ref)

    tile(i, bias_ref[0, 0], bias_ref[1, 0])

    for j in range(nb - 1):
        @pl.when(j < i)
        def _():
            tile(j, bias_ref[0, i - j] + sel_a[j], bias_ref[1, i - j] + sel_b[j])

    low_half = lax.broadcasted_iota(jnp.int32, (t, LANES), 1) < HEAD_DIM
    o_ref[0] = jnp.where(low_half, acc_ref[0] / l_ref[0], acc_ref[1] / l_ref[1]).astype(_BF16)


def _moba_attention(qkv, bias_tiles, batch, seq):
    nq = seq // ATT_TILE
    pairs = BLK_PER_GROUP
    q0, k0, v0 = 3 * BLK_PER_GROUP, 4 * BLK_PER_GROUP, 5 * BLK_PER_GROUP
    return pl.pallas_call(
        _moba_kernel,
        out_shape=jax.ShapeDtypeStruct((pairs, batch * seq, LANES), _BF16),
        grid=(pairs, batch, nq),
        in_specs=[
            pl.BlockSpec((1, ATT_TILE, LANES), lambda h, b, i: (q0 + h, b * nq + i, 0)),
            pl.BlockSpec((1, seq, LANES), lambda h, b, i: (k0 + h, b, 0)),
            pl.BlockSpec((1, seq, LANES), lambda h, b, i: (v0 + h, b, 0)),
            pl.BlockSpec((2, nq, ATT_TILE, ATT_TILE), lambda h, b, i: (2 + h, 0, 0, 0)),
        ],
        out_specs=pl.BlockSpec((1, ATT_TILE, LANES), lambda h, b, i: (h, b * nq + i, 0)),
        scratch_shapes=[
            pltpu.VMEM((2, ATT_TILE, 1), _F32),
            pltpu.VMEM((2, ATT_TILE, 1), _F32),
            pltpu.VMEM((2, ATT_TILE, LANES), _F32),
            pltpu.VMEM((LANES, LANES), _F32),
        ],
        compiler_params=pltpu.CompilerParams(
            dimension_semantics=("arbitrary", "arbitrary", "arbitrary"),
            vmem_limit_bytes=VMEM_LIMIT),
        name="moba_attn",
    )(qkv, qkv, qkv, bias_tiles)


def _ffn_kernel(tiles_per_seq, x_ref, yd_ref, ym_ref, wout_ref, g_ref, wup_ref, cw_ref, cb_ref,
                wdn_ref, o_ref, h_ref, act_ref, carry_ref, ext_ref):
    rows = x_ref.shape[0]
    first_tile = (pl.program_id(0) % tiles_per_seq) == 0
    heads = yd_ref.shape[0]
    y = jnp.concatenate([yd_ref[h] for h in range(heads)] + [ym_ref[h] for h in range(heads)],
                        axis=-1)
    x1 = x_ref[...] + _dot(y, wout_ref[...])
    ms = jnp.mean(x1 * x1, axis=-1, keepdims=True)
    h_ref[...] = (x1 * lax.rsqrt(ms + NORM_EPS) * g_ref[...]).astype(_BF16)

    def conv(u, col0, slot):
        cols = slice(col0, col0 + FFN_CHUNK)
        @pl.when(first_tile)
        def _():
            ext_ref[slot, 0:HALO, :] = jnp.zeros((HALO, FFN_CHUNK), _F32)

        @pl.when(jnp.logical_not(first_tile))
        def _():
            ext_ref[slot, 0:HALO, :] = carry_ref[:, cols]

        ext_ref[slot, HALO:HALO + rows, :] = u
        carry_ref[:, cols] = u[rows - HALO:rows, :]
        w = cw_ref[:, cols]
        return (w[0:1, :] * ext_ref[slot, HALO - 2:HALO - 2 + rows, :]
                + w[1:2, :] * ext_ref[slot, HALO - 1:HALO - 1 + rows, :]
                + w[2:3, :] * u + cb_ref[:, cols])

    h = h_ref[...]
    for c in range(D_FF // FFN_CHUNK):
        col_g = c * FFN_CHUNK
        col_u = D_FF + c * FFN_CHUNK
        gate = conv(_dot(h, wup_ref[:, col_g:col_g + FFN_CHUNK]), col_g, 0)
        up = conv(_dot(h, wup_ref[:, col_u:col_u + FFN_CHUNK]), col_u, 1)
        act = gate / (1.0 + jnp.exp(-gate)) * up
        act_ref[:, col_g:col_g + FFN_CHUNK] = act.astype(_BF16)
    o_ref[...] = x1 + _dot(act_ref[...], wdn_ref[...])


def _ffn(x2d, yd, ym, w_out, g, w_up, conv_w, conv_b, w_down, seq):
    n = x2d.shape[0]
    heads = yd.shape[0]
    const = lambda r: (0, 0)
    resident = functools.partial(pl.BlockSpec, index_map=const, pipeline_mode=pl.Buffered(1))
    return pl.pallas_call(
        functools.partial(_ffn_kernel, seq // FFN_ROWS),
        out_shape=jax.ShapeDtypeStruct((n, D_MODEL), _F32),
        grid=(n // FFN_ROWS,),
        in_specs=[
            pl.BlockSpec((FFN_ROWS, D_MODEL), lambda r: (r, 0)),
            pl.BlockSpec((heads, FFN_ROWS, LANES), lambda r: (0, r, 0)),
            pl.BlockSpec((heads, FFN_ROWS, LANES), lambda r: (0, r, 0)),
            resident((D_MODEL, D_MODEL)),
            resident((1, D_MODEL)),
            resident((D_MODEL, 2 * D_FF)),
            resident((CONV_WIDTH, 2 * D_FF)),
            resident((1, 2 * D_FF)),
            resident((D_FF, D_MODEL)),
        ],
        out_specs=pl.BlockSpec((FFN_ROWS, D_MODEL), lambda r: (r, 0)),
        scratch_shapes=[
            pltpu.VMEM((FFN_ROWS, D_MODEL), _BF16),
            pltpu.VMEM((FFN_ROWS, D_FF), _BF16),
            pltpu.VMEM((HALO, 2 * D_FF), _F32),
            pltpu.VMEM((2, HALO + FFN_ROWS, FFN_CHUNK), _F32),
        ],
        compiler_params=pltpu.CompilerParams(
            dimension_semantics=("arbitrary",), vmem_limit_bytes=VMEM_LIMIT),
        name="ffn",
    )(x2d, yd, ym, w_out, g, w_up, conv_w, conv_b, w_down)


def _distance_bucket(dist):
    n = jnp.maximum(dist, 0)
    nf = jnp.maximum(n, REL_MAX_EXACT).astype(_F32)
    large = REL_MAX_EXACT + (jnp.log(nf / REL_MAX_EXACT) / math.log(REL_MAX_DISTANCE / REL_MAX_EXACT)
                             * (REL_BUCKETS - REL_MAX_EXACT)).astype(jnp.int32)
    large = jnp.minimum(large, REL_BUCKETS - 1)
    return jnp.where(n < REL_MAX_EXACT, n, large)


def _bias_tiles(rel_bias, seq):
    t = ATT_TILE
    by_dist = rel_bias[_distance_bucket(jnp.arange(seq))]
    r = jnp.arange(t)[None, :, None]
    c = jnp.arange(t)[None, None, :]
    dist = jnp.arange(seq // t)[:, None, None] * t + r - c
    tiles = jnp.where((dist >= 0)[..., None], by_dist[jnp.maximum(dist, 0)], NEG)
    return jnp.transpose(tiles, (3, 0, 1, 2))


def kernel(x, ln_attn_g, w_in, qk_norm_g, diff_lambda, diff_subln_g, w_out, ln_ffn_g, w_up,
           conv_w, conv_b, w_down, rel_bias):
    batch, seq, d_model = x.shape
    depth = w_in.shape[0]
    assert d_model == D_MODEL and seq % ATT_TILE == 0 and ATT_TILE == MOBA_BLOCK
    assert seq % FFN_ROWS == 0 and (batch * seq) % PROJ_ROWS == 0

    bias_tiles = _bias_tiles(rel_bias.astype(_F32), seq)
    w_in_b, w_out_b = w_in.astype(_BF16), w_out.astype(_BF16)
    w_up_b, w_down_b = w_up.astype(_BF16), w_down.astype(_BF16)
    qk_gain = jnp.concatenate([qk_norm_g, qk_norm_g], axis=-1).astype(_F32)

    xs = x.reshape(batch * seq, d_model).astype(_F32)
    for i in range(depth):
        lam_init = jnp.full((1,), 0.8 - 0.6 * math.exp(-0.3 * i), _F32)
        qkv = _proj(xs, ln_attn_g[i][None, :], w_in_b[i], qk_gain[i])
        yd = _diff_attention(qkv, bias_tiles, lam_init, diff_lambda[i].astype(_F32),
                             diff_subln_g[i][None, :], batch, seq)
        ym = _moba_attention(qkv, bias_tiles, batch, seq)
        xs = _ffn(xs, yd, ym, w_out_b[i], ln_ffn_g[i][None, :], w_up_b[i], conv_w[i],
                  conv_b[i][None, :], w_down_b[i], seq)
    return xs.reshape(batch, seq, d_model).astype(x.dtype)
```

```python
import functools
import math

import jax
import jax.numpy as jnp
from jax import lax
from jax.experimental import pallas as pl
from jax.experimental.pallas import tpu as pltpu

D_MODEL = 1024
HEAD_DIM = 64
LANES = 128
N_GROUPS = 6
GROUP_WIDTH = 512
N_COLBLK = N_GROUPS * GROUP_WIDTH // LANES
BLK_PER_GROUP = GROUP_WIDTH // LANES
D_FF = 2816
CONV_WIDTH = 3
MOBA_BLOCK = 256
MOBA_TOPK = 3
REL_BUCKETS = 32
REL_MAX_EXACT = 16
REL_MAX_DISTANCE = 1024
NORM_EPS = 1e-6
NEG = -1e30

ATT_TILE = 256
GATE_ROWS = 16
PROJ_ROWS = 512
FFN_ROWS = 512
FFN_CHUNK = 256
HALO = 8
VMEM_LIMIT = 56 * 1024 * 1024

_F32 = jnp.float32
_BF16 = jnp.bfloat16


def _dot(a, b):
    return jnp.dot(a, b, preferred_element_type=_F32)


def _dot_nt(a, b):
    return lax.dot_general(a, b, (((1,), (1,)), ((), ())), preferred_element_type=_F32)


def _proj_kernel(x_ref, g_ref, w_ref, qkg_ref, o_ref):
    x = x_ref[...]
    ms = jnp.mean(x * x, axis=-1, keepdims=True)
    h = (x * lax.rsqrt(ms + NORM_EPS) * g_ref[...]).astype(_BF16)
    rows = x.shape[0]
    low_half = lax.broadcasted_iota(jnp.int32, (rows, LANES), 1) < HEAD_DIM
    norm_groups = {0: (0, HEAD_DIM ** -0.5), 1: (1, 1.0), 3: (2, HEAD_DIM ** -0.5), 4: (3, 1.0)}
    for grp in range(N_GROUPS):
        yg = _dot(h, w_ref[:, grp * GROUP_WIDTH:(grp + 1) * GROUP_WIDTH])
        for c in range(BLK_PER_GROUP):
            y = yg[:, c * LANES:(c + 1) * LANES]
            if grp in norm_groups:
                gain_row, scale = norm_groups[grp]
                sq = y * y
                tot = jnp.sum(sq, axis=-1, keepdims=True)
                lo = jnp.sum(jnp.where(low_half, sq, 0.0), axis=-1, keepdims=True)
                ms_h = jnp.where(low_half, lo, tot - lo) * (1.0 / HEAD_DIM)
                gain = qkg_ref[gain_row:gain_row + 1, :]
                y = y * lax.rsqrt(ms_h + NORM_EPS) * gain
                if scale != 1.0:
                    y = y * scale
            o_ref[grp * BLK_PER_GROUP + c] = y.astype(_BF16)


def _proj(x2d, g, w_bf16, qkg):
    n = x2d.shape[0]
    return pl.pallas_call(
        _proj_kernel,
        out_shape=jax.ShapeDtypeStruct((N_COLBLK, n, LANES), _BF16),
        grid=(n // PROJ_ROWS,),
        in_specs=[
            pl.BlockSpec((PROJ_ROWS, D_MODEL), lambda r: (r, 0)),
            pl.BlockSpec((1, D_MODEL), lambda r: (0, 0)),
            pl.BlockSpec((D_MODEL, N_GROUPS * GROUP_WIDTH), lambda r: (0, 0),
                         pipeline_mode=pl.Buffered(1)),
            pl.BlockSpec((4, LANES), lambda r: (0, 0)),
        ],
        out_specs=pl.BlockSpec((N_COLBLK, PROJ_ROWS, LANES), lambda r: (0, r, 0)),
        compiler_params=pltpu.CompilerParams(
            dimension_semantics=("arbitrary",), vmem_limit_bytes=VMEM_LIMIT),
        name="proj",
    )(x2d, g, w_bf16, qkg)


def _split_halves(q):
    low_half = lax.broadcasted_iota(jnp.int32, q.shape, 1) < HEAD_DIM
    zero = jnp.zeros_like(q)
    return jnp.where(low_half, q, zero), jnp.where(low_half, zero, q)


def _transpose_values(v_ref, vt_ref):
    for n in range(vt_ref.shape[0]):
        v_t = v_ref[0, n * ATT_TILE:(n + 1) * ATT_TILE, :].astype(_F32)
        vt_ref[n] = v_t.T.astype(_BF16)


def _online_softmax_step(s, vt, m_prev, l_prev, acc):
    m_new = jnp.maximum(m_prev, jnp.max(s, axis=0, keepdims=True))
    alpha = jnp.exp(m_prev - m_new)
    p = jnp.exp(s - m_new)
    l_new = alpha * l_prev + jnp.sum(p, axis=0, keepdims=True)
    acc[...] = alpha * acc[...] + _dot(vt, p.astype(_BF16))
    return m_new, l_new


def _softmax_init():
    row = (1, ATT_TILE)
    return (jnp.full(row, NEG, _F32), jnp.zeros(row, _F32), jnp.full(row, NEG, _F32),
            jnp.zeros(row, _F32))


def _diff_kernel(lam_init_ref, lamp_ref, q_ref, k_ref, v_ref, bias_ref, subg_ref, o_ref,
                 acc_ref, vt_ref):
    i = pl.program_id(2)
    t = ATT_TILE

    @pl.when(i == 0)
    def _():
        _transpose_values(v_ref, vt_ref)

    q_lo, q_hi = _split_halves(q_ref[0])
    acc_ref[...] = jnp.zeros(acc_ref.shape, _F32)

    def tile(j, delta, carry):
        m0, l0, m1, l1 = carry
        k_t = k_ref[0, pl.ds(pl.multiple_of(j * t, t), t), :]
        vt = vt_ref[j]
        bias = bias_ref[0, delta]
        m0, l0 = _online_softmax_step(_dot_nt(k_t, q_lo) + bias, vt, m0, l0, acc_ref.at[0])
        m1, l1 = _online_softmax_step(_dot_nt(k_t, q_hi) + bias, vt, m1, l1, acc_ref.at[1])
        return m0, l0, m1, l1

    carry = tile(i, 0, _softmax_init())
    _, l0, _, l1 = lax.fori_loop(0, i, lambda j, c: tile(j, i - j, c), carry)

    lf = lamp_ref[...]
    lam_init = lam_init_ref[0]
    lam = (jnp.exp(jnp.sum(lf[0:1, :] * lf[1:2, :], axis=-1, keepdims=True))
           - jnp.exp(jnp.sum(lf[2:3, :] * lf[3:4, :], axis=-1, keepdims=True)) + lam_init)
    out_t = acc_ref[0] / l0 - lam * (acc_ref[1] / l1)
    ms = jnp.mean(out_t * out_t, axis=0, keepdims=True)
    out = (out_t * lax.rsqrt(ms + NORM_EPS)).T * subg_ref[...] * (1.0 - lam_init)
    o_ref[0] = out.astype(_BF16)


def _diff_attention(qkv, bias_tiles, lam_init, lam_params, sub_g, batch, seq):
    nq = seq // ATT_TILE
    heads = BLK_PER_GROUP
    return pl.pallas_call(
        _diff_kernel,
        out_shape=jax.ShapeDtypeStruct((heads, batch * seq, LANES), _BF16),
        grid=(heads, batch, nq),
        in_specs=[
            pl.BlockSpec(memory_space=pltpu.SMEM),
            pl.BlockSpec((4, HEAD_DIM), lambda h, b, i: (0, 0)),
            pl.BlockSpec((1, ATT_TILE, LANES), lambda h, b, i: (h, b * nq + i, 0)),
            pl.BlockSpec((1, seq, LANES), lambda h, b, i: (BLK_PER_GROUP + h, b, 0)),
            pl.BlockSpec((1, seq, LANES), lambda h, b, i: (2 * BLK_PER_GROUP + h, b, 0)),
            pl.BlockSpec((1, nq, ATT_TILE, ATT_TILE), lambda h, b, i: (h, 0, 0, 0)),
            pl.BlockSpec((1, LANES), lambda h, b, i: (0, 0)),
        ],
        out_specs=pl.BlockSpec((1, ATT_TILE, LANES), lambda h, b, i: (h, b * nq + i, 0)),
        scratch_shapes=[
            pltpu.VMEM((2, LANES, ATT_TILE), _F32),
            pltpu.VMEM((nq, LANES, ATT_TILE), _BF16),
        ],
        compiler_params=pltpu.CompilerParams(
            dimension_semantics=("arbitrary", "arbitrary", "arbitrary"),
            vmem_limit_bytes=VMEM_LIMIT),
        name="diff_attn",
    )(lam_init, lam_params, qkv, qkv, qkv, bias_tiles, sub_g)


def _moba_kernel(q_ref, k_ref, v_ref, bias_ref, o_ref, acc_ref, vt_ref, kmean_ref, sel_ref):
    i = pl.program_id(2)
    t = ATT_TILE
    nb = k_ref.shape[1] // MOBA_BLOCK

    @pl.when(i == 0)
    def _():
        _transpose_values(v_ref, vt_ref)
        kmean_ref[...] = jnp.zeros(kmean_ref.shape, _F32)
        for n in range(nb):
            kb = k_ref[0, n * MOBA_BLOCK:(n + 1) * MOBA_BLOCK, :].astype(_F32)
            kmean_ref[n:n + 1, :] = jnp.sum(kb, axis=0, keepdims=True) * (1.0 / MOBA_BLOCK)

    q_lo, q_hi = _split_halves(q_ref[0])
    acc_ref[...] = jnp.zeros(acc_ref.shape, _F32)

    kmean = kmean_ref[...]
    km_hi = kmean.astype(_BF16)
    km_lo = (kmean - km_hi.astype(_F32)).astype(_BF16)
    block = lax.broadcasted_iota(jnp.int32, (GATE_ROWS, t), 0)
    for stream, q_half in enumerate((q_lo, q_hi)):
        gate = _dot_nt(km_hi, q_half) + _dot_nt(km_lo, q_half)
        for n in range(nb - 1):
            g_n = gate[n:n + 1, :]
            beats = ((gate > g_n) | ((gate == g_n) & (block < n))) & (block < i)
            rank = jnp.sum(jnp.where(beats, 1.0, 0.0), axis=0, keepdims=True)
            keep = (rank < MOBA_TOPK) & (jnp.abs(g_n) < jnp.inf)
            sel_ref[stream, n:n + 1, :] = jnp.where(keep, 0.0, NEG)

    half = LANES // 2

    def tile(j, delta, carry, masked):
        m0, l0, m1, l1 = carry
        k_t = k_ref[0, pl.ds(pl.multiple_of(j * t, t), t), :]
        vt = vt_ref[j]
        s0 = _dot_nt(k_t, q_lo) + bias_ref[0, delta]
        s1 = _dot_nt(k_t, q_hi) + bias_ref[1, delta]
        if masked:
            s0 = s0 + sel_ref[0, pl.ds(j, 1), :]
            s1 = s1 + sel_ref[1, pl.ds(j, 1), :]
        m0, l0 = _online_softmax_step(s0, vt[:half], m0, l0, acc_ref.at[:half])
        m1, l1 = _online_softmax_step(s1, vt[half:], m1, l1, acc_ref.at[half:])
        return m0, l0, m1, l1

    carry = tile(i, 0, _softmax_init(), False)
    _, l0, _, l1 = lax.fori_loop(0, i, lambda j, c: tile(j, i - j, c, True), carry)

    first_head = lax.broadcasted_iota(jnp.int32, (LANES, t), 0) < half
    out_t = acc_ref[...] / jnp.where(first_head, l0, l1)
    o_ref[0] = out_t.T.astype(_BF16)


def _moba_attention(qkv, bias_tiles, batch, seq):
    nq = seq // ATT_TILE
    pairs = BLK_PER_GROUP
    q0, k0, v0 = 3 * BLK_PER_GROUP, 4 * BLK_PER_GROUP, 5 * BLK_PER_GROUP
    return pl.pallas_call(
        _moba_kernel,
        out_shape=jax.ShapeDtypeStruct((pairs, batch * seq, LANES), _BF16),
        grid=(pairs, batch, nq),
        in_specs=[
            pl.BlockSpec((1, ATT_TILE, LANES), lambda h, b, i: (q0 + h, b * nq + i, 0)),
            pl.BlockSpec((1, seq, LANES), lambda h, b, i: (k0 + h, b, 0)),
            pl.BlockSpec((1, seq, LANES), lambda h, b, i: (v0 + h, b, 0)),
            pl.BlockSpec((2, nq, ATT_TILE, ATT_TILE), lambda h, b, i: (2 + h, 0, 0, 0)),
        ],
        out_specs=pl.BlockSpec((1, ATT_TILE, LANES), lambda h, b, i: (h, b * nq + i, 0)),
        scratch_shapes=[
            pltpu.VMEM((LANES, ATT_TILE), _F32),
            pltpu.VMEM((nq, LANES, ATT_TILE), _BF16),
            pltpu.VMEM((GATE_ROWS, LANES), _F32),
            pltpu.VMEM((2, GATE_ROWS, ATT_TILE), _F32),
        ],
        compiler_params=pltpu.CompilerParams(
            dimension_semantics=("arbitrary", "arbitrary", "arbitrary"),
            vmem_limit_bytes=VMEM_LIMIT),
        name="moba_attn",
    )(qkv, qkv, qkv, bias_tiles)


def _ffn_kernel(tiles_per_seq, x_ref, yd_ref, ym_ref, wout_ref, g_ref, wup_ref, cw_ref, cb_ref,
                wdn_ref, o_ref, h_ref, act_ref, carry_ref, ext_ref):
    rows = x_ref.shape[0]
    first_tile = (pl.program_id(0) % tiles_per_seq) == 0
    heads = yd_ref.shape[0]
    y = jnp.concatenate([yd_ref[h] for h in range(heads)] + [ym_ref[h] for h in range(heads)],
                        axis=-1)
    x1 = x_ref[...] + _dot(y, wout_ref[...])
    ms = jnp.mean(x1 * x1, axis=-1, keepdims=True)
    h_ref[...] = (x1 * lax.rsqrt(ms + NORM_EPS) * g_ref[...]).astype(_BF16)

    def conv(u, col0, slot):
        cols = slice(col0, col0 + FFN_CHUNK)

        @pl.when(first_tile)
        def _():
            ext_ref[slot, 0:HALO, :] = jnp.zeros((HALO, FFN_CHUNK), _F32)

        @pl.when(jnp.logical_not(first_tile))
        def _():
            ext_ref[slot, 0:HALO, :] = carry_ref[:, cols]

        ext_ref[slot, HALO:HALO + rows, :] = u
        carry_ref[:, cols] = u[rows - HALO:rows, :]
        w = cw_ref[:, cols]
        return (w[0:1, :] * ext_ref[slot, HALO - 2:HALO - 2 + rows, :]
                + w[1:2, :] * ext_ref[slot, HALO - 1:HALO - 1 + rows, :]
                + w[2:3, :] * u + cb_ref[:, cols])

    h = h_ref[...]
    for c in range(D_FF // FFN_CHUNK):
        col_g = c * FFN_CHUNK
        col_u = D_FF + c * FFN_CHUNK
        gate = conv(_dot(h, wup_ref[:, col_g:col_g + FFN_CHUNK]), col_g, 0)
        up = conv(_dot(h, wup_ref[:, col_u:col_u + FFN_CHUNK]), col_u, 1)
        act = gate / (1.0 + jnp.exp(-gate)) * up
        act_ref[:, col_g:col_g + FFN_CHUNK] = act.astype(_BF16)
    o_ref[...] = x1 + _dot(act_ref[...], wdn_ref[...])


def _ffn(x2d, yd, ym, w_out, g, w_up, conv_w, conv_b, w_down, seq):
    n = x2d.shape[0]
    heads = yd.shape[0]
    const = lambda r: (0, 0)
    resident = functools.partial(pl.BlockSpec, index_map=const, pipeline_mode=pl.Buffered(1))
    return pl.pallas_call(
        functools.partial(_ffn_kernel, seq // FFN_ROWS),
        out_shape=jax.ShapeDtypeStruct((n, D_MODEL), _F32),
        grid=(n // FFN_ROWS,),
        in_specs=[
            pl.BlockSpec((FFN_ROWS, D_MODEL), lambda r: (r, 0)),
            pl.BlockSpec((heads, FFN_ROWS, LANES), lambda r: (0, r, 0)),
            pl.BlockSpec((heads, FFN_ROWS, LANES), lambda r: (0, r, 0)),
            resident((D_MODEL, D_MODEL)),
            resident((1, D_MODEL)),
            resident((D_MODEL, 2 * D_FF)),
            resident((CONV_WIDTH, 2 * D_FF)),
            resident((1, 2 * D_FF)),
            resident((D_FF, D_MODEL)),
        ],
        out_specs=pl.BlockSpec((FFN_ROWS, D_MODEL), lambda r: (r, 0)),
        scratch_shapes=[
            pltpu.VMEM((FFN_ROWS, D_MODEL), _BF16),
            pltpu.VMEM((FFN_ROWS, D_FF), _BF16),
            pltpu.VMEM((HALO, 2 * D_FF), _F32),
            pltpu.VMEM((2, HALO + FFN_ROWS, FFN_CHUNK), _F32),
        ],
        compiler_params=pltpu.CompilerParams(
            dimension_semantics=("arbitrary",), vmem_limit_bytes=VMEM_LIMIT),
        name="ffn",
    )(x2d, yd, ym, w_out, g, w_up, conv_w, conv_b, w_down)


def _distance_bucket(dist):
    n = jnp.maximum(dist, 0)
    nf = jnp.maximum(n, REL_MAX_EXACT).astype(_F32)
    large = REL_MAX_EXACT + (jnp.log(nf / REL_MAX_EXACT) / math.log(REL_MAX_DISTANCE / REL_MAX_EXACT)
                             * (REL_BUCKETS - REL_MAX_EXACT)).astype(jnp.int32)
    large = jnp.minimum(large, REL_BUCKETS - 1)
    return jnp.where(n < REL_MAX_EXACT, n, large)


def _bias_tiles(rel_bias, seq):
    t = ATT_TILE
    by_dist = rel_bias[_distance_bucket(jnp.arange(seq))].T
    heads = by_dist.shape[0]
    ext = jnp.concatenate([jnp.full((heads, t), NEG, _F32), by_dist], axis=1)
    idx = (jnp.arange(seq // t) * t)[:, None] + jnp.arange(t)[None, :]
    w = jnp.concatenate([ext[:, t + idx], ext[:, idx]], axis=-1)
    flat = jnp.tile(w, (1, 1, t))[:, :, :t * (2 * t - 1)]
    return flat.reshape(heads, seq // t, t, 2 * t - 1)[:, :, :, :t]


def kernel(x, ln_attn_g, w_in, qk_norm_g, diff_lambda, diff_subln_g, w_out, ln_ffn_g, w_up,
           conv_w, conv_b, w_down, rel_bias):
    batch, seq, d_model = x.shape
    depth = w_in.shape[0]
    assert d_model == D_MODEL and seq % ATT_TILE == 0 and ATT_TILE == MOBA_BLOCK
    assert seq % FFN_ROWS == 0 and (batch * seq) % PROJ_ROWS == 0

    bias_tiles = _bias_tiles(rel_bias.astype(_F32), seq)
    w_in_b, w_out_b = w_in.astype(_BF16), w_out.astype(_BF16)
    w_up_b, w_down_b = w_up.astype(_BF16), w_down.astype(_BF16)
    qk_gain = jnp.concatenate([qk_norm_g, qk_norm_g], axis=-1).astype(_F32)

    xs = x.reshape(batch * seq, d_model).astype(_F32)
    for i in range(depth):
        lam_init = jnp.full((1,), 0.8 - 0.6 * math.exp(-0.3 * i), _F32)
        qkv = _proj(xs, ln_attn_g[i][None, :], w_in_b[i], qk_gain[i])
        yd = _diff_attention(qkv, bias_tiles, lam_init, diff_lambda[i].astype(_F32),
                             diff_subln_g[i][None, :], batch, seq)
        ym = _moba_attention(qkv, bias_tiles, batch, seq)
        xs = _ffn(xs, yd, ym, w_out_b[i], ln_ffn_g[i][None, :], w_up_b[i], conv_w[i],
                  conv_b[i][None, :], w_down_b[i], seq)
    return xs.reshape(batch, seq, d_model).astype(x.dtype)
```

```python
import functools
import math

import jax
import jax.numpy as jnp
from jax import lax
from jax.experimental import pallas as pl
from jax.experimental.pallas import tpu as pltpu

D_MODEL = 1024
HEAD_DIM = 64
LANES = 128
N_GROUPS = 6
GROUP_WIDTH = 512
N_COLBLK = N_GROUPS * GROUP_WIDTH // LANES
BLK_PER_GROUP = GROUP_WIDTH // LANES
D_FF = 2816
CONV_WIDTH = 3
MOBA_BLOCK = 256
MOBA_TOPK = 3
REL_BUCKETS = 32
REL_MAX_EXACT = 16
REL_MAX_DISTANCE = 1024
NORM_EPS = 1e-6
NEG = -1e30

ATT_TILE = 256
GATE_ROWS = 16
QK_AHEAD = 4
PROJ_ROWS = 512
FFN_ROWS = 512
FFN_CHUNK = 256
HALO = 8
VMEM_LIMIT = 56 * 1024 * 1024

_F32 = jnp.float32
_BF16 = jnp.bfloat16


def _dot(a, b):
    return jnp.dot(a, b, preferred_element_type=_F32)


def _dot_nt(a, b):
    return lax.dot_general(a, b, (((1,), (1,)), ((), ())), preferred_element_type=_F32)


def _proj_kernel(x_ref, g_ref, w_ref, qkg_ref, o_ref):
    x = x_ref[...]
    ms = jnp.mean(x * x, axis=-1, keepdims=True)
    h = (x * lax.rsqrt(ms + NORM_EPS) * g_ref[...]).astype(_BF16)
    rows = x.shape[0]
    low_half = lax.broadcasted_iota(jnp.int32, (rows, LANES), 1) < HEAD_DIM
    norm_groups = {0: (0, HEAD_DIM ** -0.5), 1: (1, 1.0), 3: (2, HEAD_DIM ** -0.5), 4: (3, 1.0)}
    for grp in range(N_GROUPS):
        yg = _dot(h, w_ref[:, grp * GROUP_WIDTH:(grp + 1) * GROUP_WIDTH])
        for c in range(BLK_PER_GROUP):
            y = yg[:, c * LANES:(c + 1) * LANES]
            if grp in norm_groups:
                gain_row, scale = norm_groups[grp]
                sq = y * y
                tot = jnp.sum(sq, axis=-1, keepdims=True)
                lo = jnp.sum(jnp.where(low_half, sq, 0.0), axis=-1, keepdims=True)
                ms_h = jnp.where(low_half, lo, tot - lo) * (1.0 / HEAD_DIM)
                gain = qkg_ref[gain_row:gain_row + 1, :]
                y = y * lax.rsqrt(ms_h + NORM_EPS) * gain
                if scale != 1.0:
                    y = y * scale
            o_ref[grp * BLK_PER_GROUP + c] = y.astype(_BF16)


def _proj(x2d, g, w_bf16, qkg):
    n = x2d.shape[0]
    return pl.pallas_call(
        _proj_kernel,
        out_shape=jax.ShapeDtypeStruct((N_COLBLK, n, LANES), _BF16),
        grid=(n // PROJ_ROWS,),
        in_specs=[
            pl.BlockSpec((PROJ_ROWS, D_MODEL), lambda r: (r, 0)),
            pl.BlockSpec((1, D_MODEL), lambda r: (0, 0)),
            pl.BlockSpec((D_MODEL, N_GROUPS * GROUP_WIDTH), lambda r: (0, 0),
                         pipeline_mode=pl.Buffered(1)),
            pl.BlockSpec((4, LANES), lambda r: (0, 0)),
        ],
        out_specs=pl.BlockSpec((N_COLBLK, PROJ_ROWS, LANES), lambda r: (0, r, 0)),
        compiler_params=pltpu.CompilerParams(
            dimension_semantics=("arbitrary",), vmem_limit_bytes=VMEM_LIMIT),
        name="proj",
    )(x2d, g, w_bf16, qkg)


def _split_halves(q):
    low_half = lax.broadcasted_iota(jnp.int32, q.shape, 1) < HEAD_DIM
    zero = jnp.zeros_like(q)
    return jnp.where(low_half, q, zero), jnp.where(low_half, zero, q)


def _stage_operands(q_ref, v_ref, qlo_ref, qhi_ref, vt_ref):
    q_lo, q_hi = _split_halves(q_ref[0])
    qlo_ref[...] = q_lo
    qhi_ref[...] = q_hi
    for n in range(vt_ref.shape[0]):
        v_t = v_ref[0, n * ATT_TILE:(n + 1) * ATT_TILE, :].astype(_F32)
        vt_ref[n] = v_t.T.astype(_BF16)


def _pair_slots(p, nq):
    slots = [(0, p, p, False), (1, nq - 1 - p, nq - 1 - p, False)]
    for s in range(nq - 1):
        first = s < p
        slots.append((jnp.where(first, 0, 1), jnp.where(first, p, nq - 1 - p),
                      jnp.where(first, s, s - p), True))
    return slots


def _two_pass_attention(p, nq, qlo_ref, qhi_ref, k_ref, vt_ref, bias_ref, bias_rows, sel_ref,
                        mx_ref, lsum_ref, acc_view):
    t = ATT_TILE
    slots = _pair_slots(p, nq)

    def logits(qi, jk, stream):
        rows = pl.ds(pl.multiple_of(qi * t, t), t)
        q_x = (qlo_ref, qhi_ref)[stream][rows, :]
        k_t = k_ref[0, pl.ds(pl.multiple_of(jk * t, t), t), :]
        return _dot_nt(k_t, q_x) + bias_ref[bias_rows[stream], qi - jk]

    def sel_row(slot, jk, stream):
        return sel_ref[slot, stream, pl.ds(jk, 1), :]

    mx_ref[...] = jnp.full(mx_ref.shape, NEG, _F32)
    for slot, qi, jk, past in slots:
        for stream in range(2):
            c = jnp.max(logits(qi, jk, stream), axis=0, keepdims=True)
            if past and sel_ref is not None:
                c = c + sel_row(slot, jk, stream)
            mx_ref[slot, stream] = jnp.maximum(mx_ref[slot, stream], c)

    lsum_ref[...] = jnp.zeros(lsum_ref.shape, _F32)
    items = [(slot, qi, jk, past, stream) for slot, qi, jk, past in slots for stream in range(2)]
    ahead = [logits(it[1], it[2], it[4]) for it in items[:QK_AHEAD]]
    for n, (slot, qi, jk, past, stream) in enumerate(items):
        s_cur = ahead.pop(0)
        if n + QK_AHEAD < len(items):
            nxt = items[n + QK_AHEAD]
            ahead.append(logits(nxt[1], nxt[2], nxt[4]))
        m_row = mx_ref[slot, stream]
        if past and sel_ref is not None:
            m_row = m_row - sel_row(slot, jk, stream)
        pexp = jnp.exp(s_cur - m_row)
        lsum_ref[slot, stream] += jnp.sum(pexp, axis=0, keepdims=True)
        acc, lhs = acc_view(slot, stream, vt_ref[jk])
        acc[...] += _dot(lhs, pexp.astype(_BF16))


def _diff_kernel(lam_init_ref, lamp_ref, q_ref, k_ref, v_ref, bias_ref, subg_ref, o_ref,
                 qlo_ref, qhi_ref, vt_ref, mx_ref, lsum_ref, acc_ref):
    p = pl.program_id(2)
    t = ATT_TILE
    nq = vt_ref.shape[0]

    @pl.when(p == 0)
    def _():
        _stage_operands(q_ref, v_ref, qlo_ref, qhi_ref, vt_ref)

    acc_ref[...] = jnp.zeros(acc_ref.shape, _F32)
    _two_pass_attention(p, nq, qlo_ref, qhi_ref, k_ref, vt_ref, bias_ref, (0, 0), None,
                        mx_ref, lsum_ref, lambda slot, stream, vt: (acc_ref.at[slot, stream], vt))

    lf = lamp_ref[...]
    lam_init = lam_init_ref[0]
    lam = (jnp.exp(jnp.sum(lf[0:1, :] * lf[1:2, :], axis=-1, keepdims=True))
           - jnp.exp(jnp.sum(lf[2:3, :] * lf[3:4, :], axis=-1, keepdims=True)) + lam_init)
    for slot, qi in ((0, p), (1, nq - 1 - p)):
        out_t = (acc_ref[slot, 0] / lsum_ref[slot, 0]
                 - lam * (acc_ref[slot, 1] / lsum_ref[slot, 1]))
        ms = jnp.mean(out_t * out_t, axis=0, keepdims=True)
        out = (out_t * lax.rsqrt(ms + NORM_EPS)).T * subg_ref[...] * (1.0 - lam_init)
        o_ref[0, pl.ds(pl.multiple_of(qi * t, t), t), :] = out.astype(_BF16)


def _diff_attention(qkv, bias_tiles, lam_init, lam_params, sub_g, batch, seq):
    nq = seq // ATT_TILE
    heads = BLK_PER_GROUP
    whole_seq = lambda col0: pl.BlockSpec((1, seq, LANES), lambda h, b, p: (col0 + h, b, 0))
    return pl.pallas_call(
        _diff_kernel,
        out_shape=jax.ShapeDtypeStruct((heads, batch * seq, LANES), _BF16),
        grid=(heads, batch, nq // 2),
        in_specs=[
            pl.BlockSpec(memory_space=pltpu.SMEM),
            pl.BlockSpec((4, HEAD_DIM), lambda h, b, p: (0, 0)),
            whole_seq(0),
            whole_seq(BLK_PER_GROUP),
            whole_seq(2 * BLK_PER_GROUP),
            pl.BlockSpec((1, nq, ATT_TILE, ATT_TILE), lambda h, b, p: (h, 0, 0, 0)),
            pl.BlockSpec((1, LANES), lambda h, b, p: (0, 0)),
        ],
        out_specs=whole_seq(0),
        scratch_shapes=[
            pltpu.VMEM((seq, LANES), _BF16),
            pltpu.VMEM((seq, LANES), _BF16),
            pltpu.VMEM((nq, LANES, ATT_TILE), _BF16),
            pltpu.VMEM((2, 2, 1, ATT_TILE), _F32),
            pltpu.VMEM((2, 2, 1, ATT_TILE), _F32),
            pltpu.VMEM((2, 2, LANES, ATT_TILE), _F32),
        ],
        compiler_params=pltpu.CompilerParams(
            dimension_semantics=("arbitrary", "arbitrary", "arbitrary"),
            vmem_limit_bytes=VMEM_LIMIT),
        name="diff_attn",
    )(lam_init, lam_params, qkv, qkv, qkv, bias_tiles, sub_g)


def _moba_kernel(q_ref, k_ref, v_ref, bias_ref, o_ref, qlo_ref, qhi_ref, vt_ref, mx_ref, lsum_ref,
                 acc_ref, kmean_ref, sel_ref):
    p = pl.program_id(2)
    t = ATT_TILE
    nq = vt_ref.shape[0]
    nb = k_ref.shape[1] // MOBA_BLOCK
    half = LANES // 2

    @pl.when(p == 0)
    def _():
        _stage_operands(q_ref, v_ref, qlo_ref, qhi_ref, vt_ref)
        kmean_ref[...] = jnp.zeros(kmean_ref.shape, _F32)
        for n in range(nb):
            kb = k_ref[0, n * MOBA_BLOCK:(n + 1) * MOBA_BLOCK, :].astype(_F32)
            kmean_ref[n:n + 1, :] = jnp.sum(kb, axis=0, keepdims=True) * (1.0 / MOBA_BLOCK)

    kmean = kmean_ref[...]
    km_hi = kmean.astype(_BF16)
    km_lo = (kmean - km_hi.astype(_F32)).astype(_BF16)
    block = lax.broadcasted_iota(jnp.int32, (GATE_ROWS, t), 0)
    for slot, qi in ((0, p), (1, nq - 1 - p)):
        rows = pl.ds(pl.multiple_of(qi * t, t), t)
        for stream, q_src in enumerate((qlo_ref, qhi_ref)):
            q_half = q_src[rows, :]
            gate = _dot_nt(km_hi, q_half) + _dot_nt(km_lo, q_half)
            for n in range(nb - 1):
                g_n = gate[n:n + 1, :]
                beats = ((gate > g_n) | ((gate == g_n) & (block < n))) & (block < qi)
                rank = jnp.sum(jnp.where(beats, 1.0, 0.0), axis=0, keepdims=True)
                keep = (rank < MOBA_TOPK) & (jnp.abs(g_n) < jnp.inf)
                sel_ref[slot, stream, n:n + 1, :] = jnp.where(keep, 0.0, NEG)

    def acc_view(slot, stream, vt):
        feat = slice(stream * half, (stream + 1) * half)
        return acc_ref.at[slot, feat], vt[feat]

    acc_ref[...] = jnp.zeros(acc_ref.shape, _F32)
    _two_pass_attention(p, nq, qlo_ref, qhi_ref, k_ref, vt_ref, bias_ref, (0, 1), sel_ref,
                        mx_ref, lsum_ref, acc_view)

    first_head = lax.broadcasted_iota(jnp.int32, (LANES, t), 0) < half
    for slot, qi in ((0, p), (1, nq - 1 - p)):
        out_t = acc_ref[slot] / jnp.where(first_head, lsum_ref[slot, 0], lsum_ref[slot, 1])
        o_ref[0, pl.ds(pl.multiple_of(qi * t, t), t), :] = out_t.T.astype(_BF16)


def _moba_attention(qkv, bias_tiles, batch, seq):
    nq = seq // ATT_TILE
    pairs = BLK_PER_GROUP
    whole_seq = lambda col0: pl.BlockSpec((1, seq, LANES), lambda h, b, p: (col0 + h, b, 0))
    return pl.pallas_call(
        _moba_kernel,
        out_shape=jax.ShapeDtypeStruct((pairs, batch * seq, LANES), _BF16),
        grid=(pairs, batch, nq // 2),
        in_specs=[
            whole_seq(3 * BLK_PER_GROUP),
            whole_seq(4 * BLK_PER_GROUP),
            whole_seq(5 * BLK_PER_GROUP),
            pl.BlockSpec((2, nq, ATT_TILE, ATT_TILE), lambda h, b, p: (2 + h, 0, 0, 0)),
        ],
        out_specs=whole_seq(0),
        scratch_shapes=[
            pltpu.VMEM((seq, LANES), _BF16),
            pltpu.VMEM((seq, LANES), _BF16),
            pltpu.VMEM((nq, LANES, ATT_TILE), _BF16),
            pltpu.VMEM((2, 2, 1, ATT_TILE), _F32),
            pltpu.VMEM((2, 2, 1, ATT_TILE), _F32),
            pltpu.VMEM((2, LANES, ATT_TILE), _F32),
            pltpu.VMEM((GATE_ROWS, LANES), _F32),
            pltpu.VMEM((2, 2, GATE_ROWS, ATT_TILE), _F32),
        ],
        compiler_params=pltpu.CompilerParams(
            dimension_semantics=("arbitrary", "arbitrary", "arbitrary"),
            vmem_limit_bytes=VMEM_LIMIT),
        name="moba_attn",
    )(qkv, qkv, qkv, bias_tiles)


def _ffn_kernel(tiles_per_seq, x_ref, yd_ref, ym_ref, wout_ref, g_ref, wup_ref, cw_ref, cb_ref,
                wdn_ref, o_ref, h_ref, act_ref, carry_ref, ext_ref):
    rows = x_ref.shape[0]
    heads = yd_ref.shape[0]

    @pl.when(pl.program_id(0) % tiles_per_seq == 0)
    def _():
        carry_ref[...] = jnp.zeros(carry_ref.shape, _F32)

    y = jnp.concatenate([yd_ref[h] for h in range(heads)] + [ym_ref[h] for h in range(heads)],
                        axis=-1)
    x1 = x_ref[...] + _dot(y, wout_ref[...])
    o_ref[...] = x1
    ms = jnp.mean(x1 * x1, axis=-1, keepdims=True)
    h_ref[...] = (x1 * lax.rsqrt(ms + NORM_EPS) * g_ref[...]).astype(_BF16)

    def conv(u, col0, slot):
        cols = slice(col0, col0 + FFN_CHUNK)
        ext_ref[slot, 0:HALO, :] = carry_ref[:, cols]
        ext_ref[slot, HALO:HALO + rows, :] = u
        carry_ref[:, cols] = u[rows - HALO:rows, :]
        w = cw_ref[:, cols]
        return (w[0:1, :] * ext_ref[slot, HALO - 2:HALO - 2 + rows, :]
                + w[1:2, :] * ext_ref[slot, HALO - 1:HALO - 1 + rows, :]
                + w[2:3, :] * u + cb_ref[:, cols])

    h = h_ref[...]
    for c in range(D_FF // FFN_CHUNK):
        col_g = c * FFN_CHUNK
        col_u = D_FF + c * FFN_CHUNK
        slot = 2 * (c % 2)
        gate = conv(_dot(h, wup_ref[:, col_g:col_g + FFN_CHUNK]), col_g, slot)
        up = conv(_dot(h, wup_ref[:, col_u:col_u + FFN_CHUNK]), col_u, slot + 1)
        act = gate / (1.0 + jnp.exp(-gate)) * up
        act_ref[:, col_g:col_g + FFN_CHUNK] = act.astype(_BF16)
    o_ref[...] += _dot(act_ref[...], wdn_ref[...])


def _ffn(x2d, yd, ym, w_out, g, w_up, conv_w, conv_b, w_down, seq):
    n = x2d.shape[0]
    heads = yd.shape[0]
    const = lambda r: (0, 0)
    resident = functools.partial(pl.BlockSpec, index_map=const, pipeline_mode=pl.Buffered(1))
    return pl.pallas_call(
        functools.partial(_ffn_kernel, seq // FFN_ROWS),
        out_shape=jax.ShapeDtypeStruct((n, D_MODEL), _F32),
        grid=(n // FFN_ROWS,),
        in_specs=[
            pl.BlockSpec((FFN_ROWS, D_MODEL), lambda r: (r, 0)),
            pl.BlockSpec((heads, FFN_ROWS, LANES), lambda r: (0, r, 0)),
            pl.BlockSpec((heads, FFN_ROWS, LANES), lambda r: (0, r, 0)),
            resident((D_MODEL, D_MODEL)),
            resident((1, D_MODEL)),
            resident((D_MODEL, 2 * D_FF)),
            resident((CONV_WIDTH, 2 * D_FF)),
            resident((1, 2 * D_FF)),
            resident((D_FF, D_MODEL)),
        ],
        out_specs=pl.BlockSpec((FFN_ROWS, D_MODEL), lambda r: (r, 0)),
        scratch_shapes=[
            pltpu.VMEM((FFN_ROWS, D_MODEL), _BF16),
            pltpu.VMEM((FFN_ROWS, D_FF), _BF16),
            pltpu.VMEM((HALO, 2 * D_FF), _F32),
            pltpu.VMEM((4, HALO + FFN_ROWS, FFN_CHUNK), _F32),
        ],
        compiler_params=pltpu.CompilerParams(
            dimension_semantics=("arbitrary",), vmem_limit_bytes=VMEM_LIMIT),
        name="ffn",
    )(x2d, yd, ym, w_out, g, w_up, conv_w, conv_b, w_down)


def _distance_bucket(dist):
    n = jnp.maximum(dist, 0)
    nf = jnp.maximum(n, REL_MAX_EXACT).astype(_F32)
    large = REL_MAX_EXACT + (jnp.log(nf / REL_MAX_EXACT) / math.log(REL_MAX_DISTANCE / REL_MAX_EXACT)
                             * (REL_BUCKETS - REL_MAX_EXACT)).astype(jnp.int32)
    large = jnp.minimum(large, REL_BUCKETS - 1)
    return jnp.where(n < REL_MAX_EXACT, n, large)


def _bias_tiles(rel_bias, seq):
    t = ATT_TILE
    by_dist = rel_bias[_distance_bucket(jnp.arange(seq))].T
    heads = by_dist.shape[0]
    ext = jnp.concatenate([jnp.full((heads, t), NEG, _F32), by_dist], axis=1)
    idx = (jnp.arange(seq // t) * t)[:, None] + jnp.arange(t)[None, :]
    w = jnp.concatenate([ext[:, t + idx], ext[:, idx]], axis=-1)
    flat = jnp.tile(w, (1, 1, t))[:, :, :t * (2 * t - 1)]
    return flat.reshape(heads, seq // t, t, 2 * t - 1)[:, :, :, :t]


def kernel(x, ln_attn_g, w_in, qk_norm_g, diff_lambda, diff_subln_g, w_out, ln_ffn_g, w_up,
           conv_w, conv_b, w_down, rel_bias):
    batch, seq, d_model = x.shape
    depth = w_in.shape[0]
    assert d_model == D_MODEL and seq % (2 * ATT_TILE) == 0 and ATT_TILE == MOBA_BLOCK
    assert seq % FFN_ROWS == 0 and (batch * seq) % PROJ_ROWS == 0

    bias_tiles = _bias_tiles(rel_bias.astype(_F32), seq)
    w_in_b, w_out_b = w_in.astype(_BF16), w_out.astype(_BF16)
    w_up_b, w_down_b = w_up.astype(_BF16), w_down.astype(_BF16)
    qk_gain = jnp.concatenate([qk_norm_g, qk_norm_g], axis=-1).astype(_F32)

    xs = x.reshape(batch * seq, d_model).astype(_F32)
    for i in range(depth):
        lam_init = jnp.full((1,), 0.8 - 0.6 * math.exp(-0.3 * i), _F32)
        qkv = _proj(xs, ln_attn_g[i][None, :], w_in_b[i], qk_gain[i])
        yd = _diff_attention(qkv, bias_tiles, lam_init, diff_lambda[i].astype(_F32),
                             diff_subln_g[i][None, :], batch, seq)
        ym = _moba_attention(qkv, bias_tiles, batch, seq)
        xs = _ffn(xs, yd, ym, w_out_b[i], ln_ffn_g[i][None, :], w_up_b[i], conv_w[i],
                  conv_b[i][None, :], w_down_b[i], seq)
    return xs.reshape(batch, seq, d_model).astype(x.dtype)
```

```python
import functools
import math

import jax
import jax.numpy as jnp
from jax import lax
from jax.experimental import pallas as pl
from jax.experimental.pallas import tpu as pltpu

D_MODEL = 1024
HEAD_DIM = 64
LANES = 128
N_GROUPS = 6
GROUP_WIDTH = 512
N_COLBLK = N_GROUPS * GROUP_WIDTH // LANES
BLK_PER_GROUP = GROUP_WIDTH // LANES
D_FF = 2816
CONV_WIDTH = 3
MOBA_BLOCK = 256
MOBA_TOPK = 3
REL_BUCKETS = 32
REL_MAX_EXACT = 16
REL_MAX_DISTANCE = 1024
NORM_EPS = 1e-6
NEG = -1e30

ATT_TILE = 256
GATE_ROWS = 16
ONES_ROWS = 16
LOG2E = math.log2(math.e)
QK_AHEAD = 4
PROJ_ROWS = 512
FFN_ROWS = 512
FFN_CHUNK = 256
HALO = 8
VMEM_LIMIT = 56 * 1024 * 1024

_F32 = jnp.float32
_BF16 = jnp.bfloat16


def _dot(a, b):
    return jnp.dot(a, b, preferred_element_type=_F32)


def _dot_nt(a, b):
    return lax.dot_general(a, b, (((1,), (1,)), ((), ())), preferred_element_type=_F32)


def _proj_kernel(x_ref, g_ref, w_ref, qkg_ref, o_ref):
    x = x_ref[...]
    ms = jnp.mean(x * x, axis=-1, keepdims=True)
    h = (x * lax.rsqrt(ms + NORM_EPS) * g_ref[...]).astype(_BF16)
    rows = x.shape[0]
    low_half = lax.broadcasted_iota(jnp.int32, (rows, LANES), 1) < HEAD_DIM
    q_scale = HEAD_DIM ** -0.5 * LOG2E
    norm_groups = {0: (0, q_scale), 1: (1, 1.0), 3: (2, q_scale), 4: (3, 1.0)}
    for grp in range(N_GROUPS):
        yg = _dot(h, w_ref[:, grp * GROUP_WIDTH:(grp + 1) * GROUP_WIDTH])
        for c in range(BLK_PER_GROUP):
            y = yg[:, c * LANES:(c + 1) * LANES]
            if grp in norm_groups:
                gain_row, scale = norm_groups[grp]
                sq = y * y
                tot = jnp.sum(sq, axis=-1, keepdims=True)
                lo = jnp.sum(jnp.where(low_half, sq, 0.0), axis=-1, keepdims=True)
                ms_h = jnp.where(low_half, lo, tot - lo) * (1.0 / HEAD_DIM)
                gain = qkg_ref[gain_row:gain_row + 1, :]
                y = y * lax.rsqrt(ms_h + NORM_EPS) * gain
                if scale != 1.0:
                    y = y * scale
            o_ref[grp * BLK_PER_GROUP + c] = y.astype(_BF16)


def _proj(x2d, g, w_bf16, qkg):
    n = x2d.shape[0]
    return pl.pallas_call(
        _proj_kernel,
        out_shape=jax.ShapeDtypeStruct((N_COLBLK, n, LANES), _BF16),
        grid=(n // PROJ_ROWS,),
        in_specs=[
            pl.BlockSpec((PROJ_ROWS, D_MODEL), lambda r: (r, 0)),
            pl.BlockSpec((1, D_MODEL), lambda r: (0, 0)),
            pl.BlockSpec((D_MODEL, N_GROUPS * GROUP_WIDTH), lambda r: (0, 0),
                         pipeline_mode=pl.Buffered(1)),
            pl.BlockSpec((4, LANES), lambda r: (0, 0)),
        ],
        out_specs=pl.BlockSpec((N_COLBLK, PROJ_ROWS, LANES), lambda r: (0, r, 0)),
        compiler_params=pltpu.CompilerParams(
            dimension_semantics=("arbitrary",), vmem_limit_bytes=VMEM_LIMIT),
        name="proj",
    )(x2d, g, w_bf16, qkg)


def _split_halves(q):
    low_half = lax.broadcasted_iota(jnp.int32, q.shape, 1) < HEAD_DIM
    zero = jnp.zeros_like(q)
    return jnp.where(low_half, q, zero), jnp.where(low_half, zero, q)


def _stage_operands(q_ref, v_ref, qlo_ref, qhi_ref, vt_ref, feature_groups):
    q_lo, q_hi = _split_halves(q_ref[0])
    qlo_ref[...] = q_lo
    qhi_ref[...] = q_hi
    feat = LANES // feature_groups
    ones_rows = (lax.broadcasted_iota(jnp.int32, (ONES_ROWS, ATT_TILE), 0) == 0).astype(_BF16)
    for n in range(vt_ref.shape[0]):
        v_t = v_ref[0, n * ATT_TILE:(n + 1) * ATT_TILE, :].astype(_F32).T.astype(_BF16)
        for g in range(feature_groups):
            vt_ref[n, g, 0:feat, :] = v_t[g * feat:(g + 1) * feat]
            vt_ref[n, g, feat:feat + ONES_ROWS, :] = ones_rows


def _pair_slots(p, nq):
    slots = [(0, p, p, False), (1, nq - 1 - p, nq - 1 - p, False)]
    for s in range(nq - 1):
        first = s < p
        slots.append((jnp.where(first, 0, 1), jnp.where(first, p, nq - 1 - p),
                      jnp.where(first, s, s - p), True))
    return slots


def _paired_attention(p, nq, qlo_ref, qhi_ref, k_ref, vt_ref, bias_ref, bias_rows, sel_ref,
                      mx_ref, acc_ref, value_group):
    t = ATT_TILE
    slots = _pair_slots(p, nq)

    def logits(qi, jk, stream):
        rows = pl.ds(pl.multiple_of(qi * t, t), t)
        q_x = (qlo_ref, qhi_ref)[stream][rows, :]
        k_t = k_ref[0, pl.ds(pl.multiple_of(jk * t, t), t), :]
        return _dot_nt(k_t, q_x) + bias_ref[bias_rows[stream], qi - jk]

    def sel_row(slot, jk, stream):
        return sel_ref[slot, stream, pl.ds(jk, 1), :]

    mx_ref[...] = jnp.full(mx_ref.shape, NEG, _F32)
    acc_ref[...] = jnp.zeros(acc_ref.shape, _F32)
    items = [(slot, qi, jk, past, stream) for slot, qi, jk, past in slots for stream in range(2)]
    ahead = [logits(it[1], it[2], it[4]) for it in items[:QK_AHEAD]]
    for n, (slot, qi, jk, past, stream) in enumerate(items):
        s_cur = ahead.pop(0)
        if n + QK_AHEAD < len(items):
            nxt = items[n + QK_AHEAD]
            ahead.append(logits(nxt[1], nxt[2], nxt[4]))
        masked = past and sel_ref is not None
        c = jnp.max(s_cur, axis=0, keepdims=True)
        if masked:
            c = c + sel_row(slot, jk, stream)
        m_prev = mx_ref[slot, stream]
        m_new = jnp.maximum(m_prev, c)
        alpha = jnp.exp2(m_prev - m_new)
        m_row = m_new - sel_row(slot, jk, stream) if masked else m_new
        pexp = jnp.exp2(s_cur - m_row).astype(_BF16)
        acc_ref[slot, stream] = alpha * acc_ref[slot, stream] + _dot(vt_ref[jk, value_group(stream)], pexp)
        mx_ref[slot, stream] = m_new


def _diff_kernel(lam_init_ref, lamp_ref, q_ref, k_ref, v_ref, bias_ref, subg_ref, o_ref,
                 qlo_ref, qhi_ref, vt_ref, mx_ref, acc_ref):
    p = pl.program_id(2)
    t = ATT_TILE
    nq = vt_ref.shape[0]

    @pl.when(p == 0)
    def _():
        _stage_operands(q_ref, v_ref, qlo_ref, qhi_ref, vt_ref, 1)

    _paired_attention(p, nq, qlo_ref, qhi_ref, k_ref, vt_ref, bias_ref, (0, 0), None,
                      mx_ref, acc_ref, lambda stream: 0)

    lf = lamp_ref[...]
    lam_init = lam_init_ref[0]
    lam = (jnp.exp(jnp.sum(lf[0:1, :] * lf[1:2, :], axis=-1, keepdims=True))
           - jnp.exp(jnp.sum(lf[2:3, :] * lf[3:4, :], axis=-1, keepdims=True)) + lam_init)
    for slot, qi in ((0, p), (1, nq - 1 - p)):
        a0, a1 = acc_ref[slot, 0], acc_ref[slot, 1]
        out_t = (a0[:LANES] / a0[LANES:LANES + 1]
                 - lam * (a1[:LANES] / a1[LANES:LANES + 1]))
        ms = jnp.mean(out_t * out_t, axis=0, keepdims=True)
        out = (out_t * lax.rsqrt(ms + NORM_EPS)).T * subg_ref[...] * (1.0 - lam_init)
        o_ref[0, pl.ds(pl.multiple_of(qi * t, t), t), :] = out.astype(_BF16)


def _diff_attention(qkv, bias_tiles, lam_init, lam_params, sub_g, batch, seq):
    nq = seq // ATT_TILE
    heads = BLK_PER_GROUP
    whole_seq = lambda col0: pl.BlockSpec((1, seq, LANES), lambda h, b, p: (col0 + h, b, 0))
    return pl.pallas_call(
        _diff_kernel,
        out_shape=jax.ShapeDtypeStruct((heads, batch * seq, LANES), _BF16),
        grid=(heads, batch, nq // 2),
        in_specs=[
            pl.BlockSpec(memory_space=pltpu.SMEM),
            pl.BlockSpec((4, HEAD_DIM), lambda h, b, p: (0, 0)),
            whole_seq(0),
            whole_seq(BLK_PER_GROUP),
            whole_seq(2 * BLK_PER_GROUP),
            pl.BlockSpec((1, nq, ATT_TILE, ATT_TILE), lambda h, b, p: (h, 0, 0, 0)),
            pl.BlockSpec((1, LANES), lambda h, b, p: (0, 0)),
        ],
        out_specs=whole_seq(0),
        scratch_shapes=[
            pltpu.VMEM((seq, LANES), _BF16),
            pltpu.VMEM((seq, LANES), _BF16),
            pltpu.VMEM((nq, 1, LANES + ONES_ROWS, ATT_TILE), _BF16),
            pltpu.VMEM((2, 2, 1, ATT_TILE), _F32),
            pltpu.VMEM((2, 2, LANES + ONES_ROWS, ATT_TILE), _F32),
        ],
        compiler_params=pltpu.CompilerParams(
            dimension_semantics=("arbitrary", "arbitrary", "arbitrary"),
            vmem_limit_bytes=VMEM_LIMIT),
        name="diff_attn",
    )(lam_init, lam_params, qkv, qkv, qkv, bias_tiles, sub_g)


def _moba_kernel(q_ref, k_ref, v_ref, bias_ref, o_ref, qlo_ref, qhi_ref, vt_ref, mx_ref, acc_ref,
                 kmean_ref, sel_ref):
    p = pl.program_id(2)
    t = ATT_TILE
    nq = vt_ref.shape[0]
    nb = k_ref.shape[1] // MOBA_BLOCK

    @pl.when(p == 0)
    def _():
        _stage_operands(q_ref, v_ref, qlo_ref, qhi_ref, vt_ref, 2)
        kmean_ref[...] = jnp.zeros(kmean_ref.shape, _F32)
        for n in range(nb):
            kb = k_ref[0, n * MOBA_BLOCK:(n + 1) * MOBA_BLOCK, :].astype(_F32)
            kmean_ref[n:n + 1, :] = jnp.sum(kb, axis=0, keepdims=True) * (1.0 / MOBA_BLOCK)

    kmean = kmean_ref[...]
    km_hi = kmean.astype(_BF16)
    km_lo = (kmean - km_hi.astype(_F32)).astype(_BF16)
    block = lax.broadcasted_iota(jnp.int32, (GATE_ROWS, t), 0)
    for slot, qi in ((0, p), (1, nq - 1 - p)):
        rows = pl.ds(pl.multiple_of(qi * t, t), t)
        for stream, q_src in enumerate((qlo_ref, qhi_ref)):
            q_half = q_src[rows, :]
            gate = _dot_nt(km_hi, q_half) + _dot_nt(km_lo, q_half)
            for n in range(nb - 1):
                g_n = gate[n:n + 1, :]
                beats = ((gate > g_n) | ((gate == g_n) & (block < n))) & (block < qi)
                rank = jnp.sum(jnp.where(beats, 1.0, 0.0), axis=0, keepdims=True)
                keep = (rank < MOBA_TOPK) & (jnp.abs(g_n) < jnp.inf)
                sel_ref[slot, stream, n:n + 1, :] = jnp.where(keep, 0.0, NEG)

    _paired_attention(p, nq, qlo_ref, qhi_ref, k_ref, vt_ref, bias_ref, (0, 1), sel_ref,
                      mx_ref, acc_ref, lambda stream: stream)

    for slot, qi in ((0, p), (1, nq - 1 - p)):
        heads_t = [acc_ref[slot, s, :HEAD_DIM] / acc_ref[slot, s, HEAD_DIM:HEAD_DIM + 1]
                   for s in range(2)]
        out_t = jnp.concatenate(heads_t, axis=0)
        o_ref[0, pl.ds(pl.multiple_of(qi * t, t), t), :] = out_t.T.astype(_BF16)


def _moba_attention(qkv, bias_tiles, batch, seq):
    nq = seq // ATT_TILE
    pairs = BLK_PER_GROUP
    whole_seq = lambda col0: pl.BlockSpec((1, seq, LANES), lambda h, b, p: (col0 + h, b, 0))
    return pl.pallas_call(
        _moba_kernel,
        out_shape=jax.ShapeDtypeStruct((pairs, batch * seq, LANES), _BF16),
        grid=(pairs, batch, nq // 2),
        in_specs=[
            whole_seq(3 * BLK_PER_GROUP),
            whole_seq(4 * BLK_PER_GROUP),
            whole_seq(5 * BLK_PER_GROUP),
            pl.BlockSpec((2, nq, ATT_TILE, ATT_TILE), lambda h, b, p: (2 + h, 0, 0, 0)),
        ],
        out_specs=whole_seq(0),
        scratch_shapes=[
            pltpu.VMEM((seq, LANES), _BF16),
            pltpu.VMEM((seq, LANES), _BF16),
            pltpu.VMEM((nq, 2, HEAD_DIM + ONES_ROWS, ATT_TILE), _BF16),
            pltpu.VMEM((2, 2, 1, ATT_TILE), _F32),
            pltpu.VMEM((2, 2, HEAD_DIM + ONES_ROWS, ATT_TILE), _F32),
            pltpu.VMEM((GATE_ROWS, LANES), _F32),
            pltpu.VMEM((2, 2, GATE_ROWS, ATT_TILE), _F32),
        ],
        compiler_params=pltpu.CompilerParams(
            dimension_semantics=("arbitrary", "arbitrary", "arbitrary"),
            vmem_limit_bytes=VMEM_LIMIT),
        name="moba_attn",
    )(qkv, qkv, qkv, bias_tiles)


def _ffn_kernel(tiles_per_seq, x_ref, yd_ref, ym_ref, wout_ref, g_ref, wup_ref, cw_ref, cb_ref,
                wdn_ref, o_ref, h_ref, act_ref, carry_ref, ext_ref):
    rows = x_ref.shape[0]
    heads = yd_ref.shape[0]

    @pl.when(pl.program_id(0) % tiles_per_seq == 0)
    def _():
        carry_ref[...] = jnp.zeros(carry_ref.shape, _F32)

    y = jnp.concatenate([yd_ref[h] for h in range(heads)] + [ym_ref[h] for h in range(heads)],
                        axis=-1)
    x1 = x_ref[...] + _dot(y, wout_ref[...])
    o_ref[...] = x1
    ms = jnp.mean(x1 * x1, axis=-1, keepdims=True)
    h_ref[...] = (x1 * lax.rsqrt(ms + NORM_EPS) * g_ref[...]).astype(_BF16)

    def conv(u, col0, slot):
        cols = slice(col0, col0 + FFN_CHUNK)
        ext_ref[slot, 0:HALO, :] = carry_ref[:, cols]
        ext_ref[slot, HALO:HALO + rows, :] = u
        carry_ref[:, cols] = u[rows - HALO:rows, :]
        w = cw_ref[:, cols]
        return (w[0:1, :] * ext_ref[slot, HALO - 2:HALO - 2 + rows, :]
                + w[1:2, :] * ext_ref[slot, HALO - 1:HALO - 1 + rows, :]
                + w[2:3, :] * u + cb_ref[:, cols])

    h = h_ref[...]
    n_chunks = D_FF // FFN_CHUNK
    head_cols = (n_chunks - 2) * FFN_CHUNK
    for c in range(n_chunks):
        col_g = c * FFN_CHUNK
        col_u = D_FF + c * FFN_CHUNK
        slot = 2 * (c % 2)
        gate_raw = _dot(h, wup_ref[:, col_g:col_g + FFN_CHUNK])
        up_raw = _dot(h, wup_ref[:, col_u:col_u + FFN_CHUNK])
        if c == n_chunks - 1:
            o_ref[...] += _dot(act_ref[:, :head_cols], wdn_ref[:head_cols, :])
        gate = conv(gate_raw, col_g, slot)
        up = conv(up_raw, col_u, slot + 1)
        act = gate / (1.0 + jnp.exp(-gate)) * up
        act_ref[:, col_g:col_g + FFN_CHUNK] = act.astype(_BF16)
    o_ref[...] += _dot(act_ref[:, head_cols:], wdn_ref[head_cols:, :])


def _ffn(x2d, yd, ym, w_out, g, w_up, conv_w, conv_b, w_down, seq):
    n = x2d.shape[0]
    heads = yd.shape[0]
    const = lambda r: (0, 0)
    resident = functools.partial(pl.BlockSpec, index_map=const, pipeline_mode=pl.Buffered(1))
    return pl.pallas_call(
        functools.partial(_ffn_kernel, seq // FFN_ROWS),
        out_shape=jax.ShapeDtypeStruct((n, D_MODEL), _F32),
        grid=(n // FFN_ROWS,),
        in_specs=[
            pl.BlockSpec((FFN_ROWS, D_MODEL), lambda r: (r, 0)),
            pl.BlockSpec((heads, FFN_ROWS, LANES), lambda r: (0, r, 0)),
            pl.BlockSpec((heads, FFN_ROWS, LANES), lambda r: (0, r, 0)),
            resident((D_MODEL, D_MODEL)),
            resident((1, D_MODEL)),
            resident((D_MODEL, 2 * D_FF)),
            resident((CONV_WIDTH, 2 * D_FF)),
            resident((1, 2 * D_FF)),
            resident((D_FF, D_MODEL)),
        ],
        out_specs=pl.BlockSpec((FFN_ROWS, D_MODEL), lambda r: (r, 0)),
        scratch_shapes=[
            pltpu.VMEM((FFN_ROWS, D_MODEL), _BF16),
            pltpu.VMEM((FFN_ROWS, D_FF), _BF16),
            pltpu.VMEM((HALO, 2 * D_FF), _F32),
            pltpu.VMEM((4, HALO + FFN_ROWS, FFN_CHUNK), _F32),
        ],
        compiler_params=pltpu.CompilerParams(
            dimension_semantics=("arbitrary",), vmem_limit_bytes=VMEM_LIMIT),
        name="ffn",
    )(x2d, yd, ym, w_out, g, w_up, conv_w, conv_b, w_down)


def _distance_bucket(dist):
    n = jnp.maximum(dist, 0)
    nf = jnp.maximum(n, REL_MAX_EXACT).astype(_F32)
    large = REL_MAX_EXACT + (jnp.log(nf / REL_MAX_EXACT) / math.log(REL_MAX_DISTANCE / REL_MAX_EXACT)
                             * (REL_BUCKETS - REL_MAX_EXACT)).astype(jnp.int32)
    large = jnp.minimum(large, REL_BUCKETS - 1)
    return jnp.where(n < REL_MAX_EXACT, n, large)


def _bias_tiles(rel_bias, seq):
    t = ATT_TILE
    by_dist = rel_bias[_distance_bucket(jnp.arange(seq))].T * LOG2E
    heads = by_dist.shape[0]
    ext = jnp.concatenate([jnp.full((heads, t), NEG, _F32), by_dist], axis=1)
    idx = (jnp.arange(seq // t) * t)[:, None] + jnp.arange(t)[None, :]
    w = jnp.concatenate([ext[:, t + idx], ext[:, idx]], axis=-1)
    flat = jnp.tile(w, (1, 1, t))[:, :, :t * (2 * t - 1)]
    return flat.reshape(heads, seq // t, t, 2 * t - 1)[:, :, :, :t]


def kernel(x, ln_attn_g, w_in, qk_norm_g, diff_lambda, diff_subln_g, w_out, ln_ffn_g, w_up,
           conv_w, conv_b, w_down, rel_bias):
    batch, seq, d_model = x.shape
    depth = w_in.shape[0]
    assert d_model == D_MODEL and seq % (2 * ATT_TILE) == 0 and ATT_TILE == MOBA_BLOCK
    assert seq % FFN_ROWS == 0 and (batch * seq) % PROJ_ROWS == 0

    bias_tiles = _bias_tiles(rel_bias.astype(_F32), seq)
    w_in_b, w_out_b = w_in.astype(_BF16), w_out.astype(_BF16)
    w_up_b, w_down_b = w_up.astype(_BF16), w_down.astype(_BF16)
    qk_gain = jnp.concatenate([qk_norm_g, qk_norm_g], axis=-1).astype(_F32)

    xs = x.reshape(batch * seq, d_model).astype(_F32)
    for i in range(depth):
        lam_init = jnp.full((1,), 0.8 - 0.6 * math.exp(-0.3 * i), _F32)
        qkv = _proj(xs, ln_attn_g[i][None, :], w_in_b[i], qk_gain[i])
        yd = _diff_attention(qkv, bias_tiles, lam_init, diff_lambda[i].astype(_F32),
                             diff_subln_g[i][None, :], batch, seq)
        ym = _moba_attention(qkv, bias_tiles, batch, seq)
        xs = _ffn(xs, yd, ym, w_out_b[i], ln_ffn_g[i][None, :], w_up_b[i], conv_w[i],
                  conv_b[i][None, :], w_down_b[i], seq)
    return xs.reshape(batch, seq, d_model).astype(x.dtype)
```

```python
import functools
import math

import jax
import jax.numpy as jnp
from jax import lax
from jax.experimental import pallas as pl
from jax.experimental.pallas import tpu as pltpu

D_MODEL = 1024
HEAD_DIM = 64
LANES = 128
N_GROUPS = 6
GROUP_WIDTH = 512
N_COLBLK = N_GROUPS * GROUP_WIDTH // LANES
BLK_PER_GROUP = GROUP_WIDTH // LANES
D_FF = 2816
CONV_WIDTH = 3
MOBA_BLOCK = 256
MOBA_TOPK = 3
REL_BUCKETS = 32
REL_MAX_EXACT = 16
REL_MAX_DISTANCE = 1024
NORM_EPS = 1e-6
NEG = -1e30

ATT_TILE = 256
GATE_ROWS = 16
ONES_ROWS = 16
LOG2E = math.log2(math.e)
QK_AHEAD = 4
PROJ_ROWS = 512
FFN_ROWS = 512
FFN_CHUNK = 256
HALO = 8
VMEM_LIMIT = 56 * 1024 * 1024

_F32 = jnp.float32
_BF16 = jnp.bfloat16


def _dot(a, b):
    return jnp.dot(a, b, preferred_element_type=_F32)


def _dot_nt(a, b):
    return lax.dot_general(a, b, (((1,), (1,)), ((), ())), preferred_element_type=_F32)


def _proj_kernel(x_ref, g_ref, w_ref, qkg_ref, o_ref):
    x = x_ref[...]
    ms = jnp.mean(x * x, axis=-1, keepdims=True)
    h = (x * lax.rsqrt(ms + NORM_EPS) * g_ref[...]).astype(_BF16)
    rows = x.shape[0]
    low_half = lax.broadcasted_iota(jnp.int32, (rows, LANES), 1) < HEAD_DIM
    q_scale = HEAD_DIM ** -0.5 * LOG2E
    norm_groups = {0: (0, q_scale), 1: (1, 1.0), 3: (2, q_scale), 4: (3, 1.0)}
    for grp in range(N_GROUPS):
        yg = _dot(h, w_ref[:, grp * GROUP_WIDTH:(grp + 1) * GROUP_WIDTH])
        for c in range(BLK_PER_GROUP):
            y = yg[:, c * LANES:(c + 1) * LANES]
            if grp in norm_groups:
                gain_row, scale = norm_groups[grp]
                sq = y * y
                tot = jnp.sum(sq, axis=-1, keepdims=True)
                lo = jnp.sum(jnp.where(low_half, sq, 0.0), axis=-1, keepdims=True)
                ms_h = jnp.where(low_half, lo, tot - lo) * (1.0 / HEAD_DIM)
                gain = qkg_ref[gain_row:gain_row + 1, :]
                y = y * lax.rsqrt(ms_h + NORM_EPS) * gain
                if scale != 1.0:
                    y = y * scale
            o_ref[grp * BLK_PER_GROUP + c] = y.astype(_BF16)


def _proj(x2d, g, w_stack, layer, qkg):
    n = x2d.shape[0]
    return pl.pallas_call(
        _proj_kernel,
        out_shape=jax.ShapeDtypeStruct((N_COLBLK, n, LANES), _BF16),
        grid=(n // PROJ_ROWS,),
        in_specs=[
            pl.BlockSpec((PROJ_ROWS, D_MODEL), lambda r: (r, 0)),
            pl.BlockSpec((1, D_MODEL), lambda r: (0, 0)),
            pl.BlockSpec((None, D_MODEL, N_GROUPS * GROUP_WIDTH), lambda r: (layer, 0, 0),
                         pipeline_mode=pl.Buffered(1)),
            pl.BlockSpec((4, LANES), lambda r: (0, 0)),
        ],
        out_specs=pl.BlockSpec((N_COLBLK, PROJ_ROWS, LANES), lambda r: (0, r, 0)),
        compiler_params=pltpu.CompilerParams(
            dimension_semantics=("arbitrary",), vmem_limit_bytes=VMEM_LIMIT),
        name="proj",
    )(x2d, g, w_stack, qkg)


def _split_halves(q):
    low_half = lax.broadcasted_iota(jnp.int32, q.shape, 1) < HEAD_DIM
    zero = jnp.zeros_like(q)
    return jnp.where(low_half, q, zero), jnp.where(low_half, zero, q)


def _stage_operands(q_ref, v_ref, qlo_ref, qhi_ref, vt_ref, feature_groups):
    q_lo, q_hi = _split_halves(q_ref[0])
    qlo_ref[...] = q_lo
    qhi_ref[...] = q_hi
    feat = LANES // feature_groups
    ones_rows = (lax.broadcasted_iota(jnp.int32, (ONES_ROWS, ATT_TILE), 0) == 0).astype(_BF16)
    for n in range(vt_ref.shape[0]):
        v_t = v_ref[0, n * ATT_TILE:(n + 1) * ATT_TILE, :].astype(_F32).T.astype(_BF16)
        for g in range(feature_groups):
            vt_ref[n, g, 0:feat, :] = v_t[g * feat:(g + 1) * feat]
            vt_ref[n, g, feat:feat + ONES_ROWS, :] = ones_rows


def _transposed_out_spec(nq):
    return pl.BlockSpec((1, nq, LANES, ATT_TILE), lambda h, b, p: (h, b, 0, 0))


def _pair_slots(p, nq):
    slots = [(0, p, p, False), (1, nq - 1 - p, nq - 1 - p, False)]
    for s in range(nq - 1):
        first = s < p
        slots.append((jnp.where(first, 0, 1), jnp.where(first, p, nq - 1 - p),
                      jnp.where(first, s, s - p), True))
    return slots


def _paired_attention(p, nq, qlo_ref, qhi_ref, k_ref, vt_ref, bias_ref, bias_rows, sel_ref,
                      mx_ref, acc_ref, value_group):
    t = ATT_TILE
    slots = _pair_slots(p, nq)

    def logits(qi, jk, stream):
        rows = pl.ds(pl.multiple_of(qi * t, t), t)
        q_x = (qlo_ref, qhi_ref)[stream][rows, :]
        k_t = k_ref[0, pl.ds(pl.multiple_of(jk * t, t), t), :]
        return _dot_nt(k_t, q_x) + bias_ref[bias_rows[stream], qi - jk]

    def sel_row(slot, jk, stream):
        return sel_ref[slot, stream, pl.ds(jk, 1), :]

    mx_ref[...] = jnp.full(mx_ref.shape, NEG, _F32)
    acc_ref[...] = jnp.zeros(acc_ref.shape, _F32)
    items = [(slot, qi, jk, past, stream) for slot, qi, jk, past in slots for stream in range(2)]
    ahead = [logits(it[1], it[2], it[4]) for it in items[:QK_AHEAD]]
    for n, (slot, qi, jk, past, stream) in enumerate(items):
        s_cur = ahead.pop(0)
        if n + QK_AHEAD < len(items):
            nxt = items[n + QK_AHEAD]
            ahead.append(logits(nxt[1], nxt[2], nxt[4]))
        masked = past and sel_ref is not None
        c = jnp.max(s_cur, axis=0, keepdims=True)
        if masked:
            c = c + sel_row(slot, jk, stream)
        m_prev = mx_ref[slot, stream]
        m_new = jnp.maximum(m_prev, c)
        alpha = jnp.exp2(m_prev - m_new)
        m_row = m_new - sel_row(slot, jk, stream) if masked else m_new
        pexp = jnp.exp2(s_cur - m_row).astype(_BF16)
        acc_ref[slot, stream] = alpha * acc_ref[slot, stream] + _dot(vt_ref[jk, value_group(stream)], pexp)
        mx_ref[slot, stream] = m_new


def _diff_kernel(lam_init_ref, lamp_ref, q_ref, k_ref, v_ref, bias_ref, subg_ref, o_ref,
                 qlo_ref, qhi_ref, vt_ref, mx_ref, acc_ref):
    p = pl.program_id(2)
    t = ATT_TILE
    nq = vt_ref.shape[0]

    @pl.when(p == 0)
    def _():
        _stage_operands(q_ref, v_ref, qlo_ref, qhi_ref, vt_ref, 1)

    _paired_attention(p, nq, qlo_ref, qhi_ref, k_ref, vt_ref, bias_ref, (0, 0), None,
                      mx_ref, acc_ref, lambda stream: 0)

    lf = lamp_ref[...]
    lam_init = lam_init_ref[0]
    lam = (jnp.exp(jnp.sum(lf[0:1, :] * lf[1:2, :], axis=-1, keepdims=True))
           - jnp.exp(jnp.sum(lf[2:3, :] * lf[3:4, :], axis=-1, keepdims=True)) + lam_init)
    for slot, qi in ((0, p), (1, nq - 1 - p)):
        a0, a1 = acc_ref[slot, 0], acc_ref[slot, 1]
        out_t = (a0[:LANES] / a0[LANES:LANES + 1]
                 - lam * (a1[:LANES] / a1[LANES:LANES + 1]))
        ms = jnp.mean(out_t * out_t, axis=0, keepdims=True)
        out_t = out_t * lax.rsqrt(ms + NORM_EPS) * subg_ref[...] * (1.0 - lam_init)
        o_ref[0, qi] = out_t.astype(_BF16)


def _diff_attention(qkv, bias_tiles, lam_init, lam_params, sub_g, batch, seq):
    nq = seq // ATT_TILE
    heads = BLK_PER_GROUP
    whole_seq = lambda col0: pl.BlockSpec((1, seq, LANES), lambda h, b, p: (col0 + h, b, 0))
    return pl.pallas_call(
        _diff_kernel,
        out_shape=jax.ShapeDtypeStruct((heads, batch * nq, LANES, ATT_TILE), _BF16),
        grid=(heads, batch, nq // 2),
        in_specs=[
            pl.BlockSpec(memory_space=pltpu.SMEM),
            pl.BlockSpec((4, HEAD_DIM), lambda h, b, p: (0, 0)),
            whole_seq(0),
            whole_seq(BLK_PER_GROUP),
            whole_seq(2 * BLK_PER_GROUP),
            pl.BlockSpec((1, nq, ATT_TILE, ATT_TILE), lambda h, b, p: (h, 0, 0, 0)),
            pl.BlockSpec((LANES, 1), lambda h, b, p: (0, 0)),
        ],
        out_specs=_transposed_out_spec(nq),
        scratch_shapes=[
            pltpu.VMEM((seq, LANES), _BF16),
            pltpu.VMEM((seq, LANES), _BF16),
            pltpu.VMEM((nq, 1, LANES + ONES_ROWS, ATT_TILE), _BF16),
            pltpu.VMEM((2, 2, 1, ATT_TILE), _F32),
            pltpu.VMEM((2, 2, LANES + ONES_ROWS, ATT_TILE), _F32),
        ],
        compiler_params=pltpu.CompilerParams(
            dimension_semantics=("arbitrary", "arbitrary", "arbitrary"),
            vmem_limit_bytes=VMEM_LIMIT),
        name="diff_attn",
    )(lam_init, lam_params, qkv, qkv, qkv, bias_tiles, sub_g)


def _moba_kernel(q_ref, k_ref, v_ref, bias_ref, o_ref, qlo_ref, qhi_ref, vt_ref, mx_ref, acc_ref,
                 kmean_ref, sel_ref):
    p = pl.program_id(2)
    t = ATT_TILE
    nq = vt_ref.shape[0]
    nb = k_ref.shape[1] // MOBA_BLOCK

    @pl.when(p == 0)
    def _():
        _stage_operands(q_ref, v_ref, qlo_ref, qhi_ref, vt_ref, 2)
        kmean_ref[...] = jnp.zeros(kmean_ref.shape, _F32)
        for n in range(nb):
            kb = k_ref[0, n * MOBA_BLOCK:(n + 1) * MOBA_BLOCK, :].astype(_F32)
            kmean_ref[n:n + 1, :] = jnp.sum(kb, axis=0, keepdims=True) * (1.0 / MOBA_BLOCK)

    kmean = kmean_ref[...]
    km_hi = kmean.astype(_BF16)
    km_lo = (kmean - km_hi.astype(_F32)).astype(_BF16)
    block = lax.broadcasted_iota(jnp.int32, (GATE_ROWS, t), 0)
    for slot, qi in ((0, p), (1, nq - 1 - p)):
        rows = pl.ds(pl.multiple_of(qi * t, t), t)
        for stream, q_src in enumerate((qlo_ref, qhi_ref)):
            q_half = q_src[rows, :]
            gate = _dot_nt(km_hi, q_half) + _dot_nt(km_lo, q_half)
            for n in range(nb - 1):
                g_n = gate[n:n + 1, :]
                beats = ((gate > g_n) | ((gate == g_n) & (block < n))) & (block < qi)
                rank = jnp.sum(jnp.where(beats, 1.0, 0.0), axis=0, keepdims=True)
                keep = (rank < MOBA_TOPK) & (jnp.abs(g_n) < jnp.inf)
                sel_ref[slot, stream, n:n + 1, :] = jnp.where(keep, 0.0, NEG)

    _paired_attention(p, nq, qlo_ref, qhi_ref, k_ref, vt_ref, bias_ref, (0, 1), sel_ref,
                      mx_ref, acc_ref, lambda stream: stream)

    for slot, qi in ((0, p), (1, nq - 1 - p)):
        heads_t = [acc_ref[slot, s, :HEAD_DIM] / acc_ref[slot, s, HEAD_DIM:HEAD_DIM + 1]
                   for s in range(2)]
        out_t = jnp.concatenate(heads_t, axis=0)
        o_ref[0, qi] = out_t.astype(_BF16)


def _moba_attention(qkv, bias_tiles, batch, seq):
    nq = seq // ATT_TILE
    pairs = BLK_PER_GROUP
    whole_seq = lambda col0: pl.BlockSpec((1, seq, LANES), lambda h, b, p: (col0 + h, b, 0))
    return pl.pallas_call(
        _moba_kernel,
        out_shape=jax.ShapeDtypeStruct((pairs, batch * nq, LANES, ATT_TILE), _BF16),
        grid=(pairs, batch, nq // 2),
        in_specs=[
            whole_seq(3 * BLK_PER_GROUP),
            whole_seq(4 * BLK_PER_GROUP),
            whole_seq(5 * BLK_PER_GROUP),
            pl.BlockSpec((2, nq, ATT_TILE, ATT_TILE), lambda h, b, p: (2 + h, 0, 0, 0)),
        ],
        out_specs=_transposed_out_spec(nq),
        scratch_shapes=[
            pltpu.VMEM((seq, LANES), _BF16),
            pltpu.VMEM((seq, LANES), _BF16),
            pltpu.VMEM((nq, 2, HEAD_DIM + ONES_ROWS, ATT_TILE), _BF16),
            pltpu.VMEM((2, 2, 1, ATT_TILE), _F32),
            pltpu.VMEM((2, 2, HEAD_DIM + ONES_ROWS, ATT_TILE), _F32),
            pltpu.VMEM((GATE_ROWS, LANES), _F32),
            pltpu.VMEM((2, 2, GATE_ROWS, ATT_TILE), _F32),
        ],
        compiler_params=pltpu.CompilerParams(
            dimension_semantics=("arbitrary", "arbitrary", "arbitrary"),
            vmem_limit_bytes=VMEM_LIMIT),
        name="moba_attn",
    )(qkv, qkv, qkv, bias_tiles)


def _ffn_kernel(tiles_per_seq, x_ref, yd_ref, ym_ref, wout_ref, g_ref, wup_ref, cw_ref, cb_ref,
                wdn_ref, o_ref, h_ref, act_ref, carry_ref, ext_ref):
    rows = x_ref.shape[0]
    heads = yd_ref.shape[0]

    @pl.when(pl.program_id(0) % tiles_per_seq == 0)
    def _():
        carry_ref[...] = jnp.zeros(carry_ref.shape, _F32)

    y_t = jnp.concatenate(
        [jnp.concatenate([y_ref[h, j] for j in range(y_ref.shape[1])], axis=-1)
         for y_ref in (yd_ref, ym_ref) for h in range(heads)], axis=0)
    x1 = x_ref[...] + lax.dot_general(y_t, wout_ref[...], (((0,), (0,)), ((), ())),
                                      preferred_element_type=_F32)
    o_ref[...] = x1
    ms = jnp.mean(x1 * x1, axis=-1, keepdims=True)
    h_ref[...] = (x1 * lax.rsqrt(ms + NORM_EPS) * g_ref[...]).astype(_BF16)

    def conv(u, col0, slot):
        cols = slice(col0, col0 + FFN_CHUNK)
        ext_ref[slot, 0:HALO, :] = carry_ref[:, cols]
        ext_ref[slot, HALO:HALO + rows, :] = u
        carry_ref[:, cols] = u[rows - HALO:rows, :]
        w = cw_ref[:, cols]
        return (w[0:1, :] * ext_ref[slot, HALO - 2:HALO - 2 + rows, :]
                + w[1:2, :] * ext_ref[slot, HALO - 1:HALO - 1 + rows, :]
                + w[2:3, :] * u + cb_ref[:, cols])

    h = h_ref[...]
    n_chunks = D_FF // FFN_CHUNK
    head_cols = (n_chunks - 2) * FFN_CHUNK
    for c in range(n_chunks):
        col_g = c * FFN_CHUNK
        col_u = D_FF + c * FFN_CHUNK
        slot = 2 * (c % 2)
        gate_raw = _dot(h, wup_ref[:, col_g:col_g + FFN_CHUNK])
        up_raw = _dot(h, wup_ref[:, col_u:col_u + FFN_CHUNK])
        if c == n_chunks - 1:
            o_ref[...] += _dot(act_ref[:, :head_cols], wdn_ref[:head_cols, :])
        gate = conv(gate_raw, col_g, slot)
        up = conv(up_raw, col_u, slot + 1)
        act = gate / (1.0 + jnp.exp(-gate)) * up
        act_ref[:, col_g:col_g + FFN_CHUNK] = act.astype(_BF16)
    o_ref[...] += _dot(act_ref[:, head_cols:], wdn_ref[head_cols:, :])


def _ffn(x2d, yd, ym, w_out, g, w_up, conv_w, conv_b, w_down, layer, seq):
    n = x2d.shape[0]
    heads = yd.shape[0]
    resident = functools.partial(pl.BlockSpec, index_map=lambda r: (0, 0),
                                 pipeline_mode=pl.Buffered(1))
    layer_slab = lambda rows, cols: pl.BlockSpec((None, rows, cols), lambda r: (layer, 0, 0),
                                                 pipeline_mode=pl.Buffered(1))
    return pl.pallas_call(
        functools.partial(_ffn_kernel, seq // FFN_ROWS),
        out_shape=jax.ShapeDtypeStruct((n, D_MODEL), _F32),
        grid=(n // FFN_ROWS,),
        in_specs=[
            pl.BlockSpec((FFN_ROWS, D_MODEL), lambda r: (r, 0)),
            pl.BlockSpec((heads, FFN_ROWS // ATT_TILE, LANES, ATT_TILE), lambda r: (0, r, 0, 0)),
            pl.BlockSpec((heads, FFN_ROWS // ATT_TILE, LANES, ATT_TILE), lambda r: (0, r, 0, 0)),
            layer_slab(D_MODEL, D_MODEL),
            resident((1, D_MODEL)),
            layer_slab(D_MODEL, 2 * D_FF),
            resident((CONV_WIDTH, 2 * D_FF)),
            resident((1, 2 * D_FF)),
            layer_slab(D_FF, D_MODEL),
        ],
        out_specs=pl.BlockSpec((FFN_ROWS, D_MODEL), lambda r: (r, 0)),
        scratch_shapes=[
            pltpu.VMEM((FFN_ROWS, D_MODEL), _BF16),
            pltpu.VMEM((FFN_ROWS, D_FF), _BF16),
            pltpu.VMEM((HALO, 2 * D_FF), _F32),
            pltpu.VMEM((4, HALO + FFN_ROWS, FFN_CHUNK), _F32),
        ],
        compiler_params=pltpu.CompilerParams(
            dimension_semantics=("arbitrary",), vmem_limit_bytes=VMEM_LIMIT),
        name="ffn",
    )(x2d, yd, ym, w_out, g, w_up, conv_w, conv_b, w_down)


def _distance_bucket(dist):
    n = jnp.maximum(dist, 0)
    nf = jnp.maximum(n, REL_MAX_EXACT).astype(_F32)
    large = REL_MAX_EXACT + (jnp.log(nf / REL_MAX_EXACT) / math.log(REL_MAX_DISTANCE / REL_MAX_EXACT)
                             * (REL_BUCKETS - REL_MAX_EXACT)).astype(jnp.int32)
    large = jnp.minimum(large, REL_BUCKETS - 1)
    return jnp.where(n < REL_MAX_EXACT, n, large)


def _bias_tile_kernel(w_ref, o_ref):
    t = ATT_TILE
    for d in range(o_ref.shape[1]):
        rows = jnp.broadcast_to(w_ref[0, d:d + 1, :], (t, 2 * t))
        o_ref[0, d] = pltpu.roll(rows, 0, 1, stride=1, stride_axis=0)[:, :t]


def _bias_tiles(rel_bias, seq):
    t = ATT_TILE
    nt = seq // t
    by_dist = rel_bias[_distance_bucket(jnp.arange(seq))].T * LOG2E
    heads = by_dist.shape[0]
    ext = jnp.concatenate([jnp.full((heads, t), NEG, _F32), by_dist], axis=1)
    idx = (jnp.arange(nt) * t)[:, None] + jnp.arange(t)[None, :]
    w = jnp.concatenate([ext[:, t + idx], ext[:, idx]], axis=-1)
    return pl.pallas_call(
        _bias_tile_kernel,
        out_shape=jax.ShapeDtypeStruct((heads, nt, t, t), _F32),
        grid=(heads,),
        in_specs=[pl.BlockSpec((1, nt, 2 * t), lambda h: (h, 0, 0))],
        out_specs=pl.BlockSpec((1, nt, t, t), lambda h: (h, 0, 0, 0)),
        compiler_params=pltpu.CompilerParams(dimension_semantics=("arbitrary",)),
        name="bias_tiles",
    )(w)


def kernel(x, ln_attn_g, w_in, qk_norm_g, diff_lambda, diff_subln_g, w_out, ln_ffn_g, w_up,
           conv_w, conv_b, w_down, rel_bias):
    batch, seq, d_model = x.shape
    depth = w_in.shape[0]
    assert d_model == D_MODEL and seq % (2 * ATT_TILE) == 0 and ATT_TILE == MOBA_BLOCK
    assert seq % FFN_ROWS == 0 and FFN_ROWS % ATT_TILE == 0 and (batch * seq) % PROJ_ROWS == 0

    bias_tiles = _bias_tiles(rel_bias.astype(_F32), seq)
    w_in_b, w_out_b = w_in.astype(_BF16), w_out.astype(_BF16)
    w_up_b, w_down_b = w_up.astype(_BF16), w_down.astype(_BF16)
    qk_gain = jnp.concatenate([qk_norm_g, qk_norm_g], axis=-1).astype(_F32)

    xs = x.reshape(batch * seq, d_model).astype(_F32)
    for i in range(depth):
        lam_init = jnp.full((1,), 0.8 - 0.6 * math.exp(-0.3 * i), _F32)
        qkv = _proj(xs, ln_attn_g[i][None, :], w_in_b, i, qk_gain[i])
        yd = _diff_attention(qkv, bias_tiles, lam_init, diff_lambda[i].astype(_F32),
                             diff_subln_g[i][:, None], batch, seq)
        ym = _moba_attention(qkv, bias_tiles, batch, seq)
        xs = _ffn(xs, yd, ym, w_out_b, ln_ffn_g[i][None, :], w_up_b, conv_w[i],
                  conv_b[i][None, :], w_down_b, i, seq)
    return xs.reshape(batch, seq, d_model).astype(x.dtype)
```

```python
import functools
import math

import jax
import jax.numpy as jnp
from jax import lax
from jax.experimental import pallas as pl
from jax.experimental.pallas import tpu as pltpu

D_MODEL = 1024
HEAD_DIM = 64
LANES = 128
N_GROUPS = 6
GROUP_WIDTH = 512
N_COLBLK = N_GROUPS * GROUP_WIDTH // LANES
BLK_PER_GROUP = GROUP_WIDTH // LANES
D_FF = 2816
CONV_WIDTH = 3
MOBA_BLOCK = 256
MOBA_TOPK = 3
REL_BUCKETS = 32
REL_MAX_EXACT = 16
REL_MAX_DISTANCE = 1024
NORM_EPS = 1e-6
NEG = -1e30

ATT_TILE = 256
GATE_ROWS = 16
ONES_ROWS = 16
LOG2E = math.log2(math.e)
QK_AHEAD = 4
PROJ_ROWS = 512
FFN_ROWS = 512
FFN_CHUNK = 256
HALO = 8
VMEM_LIMIT = 56 * 1024 * 1024

_F32 = jnp.float32
_BF16 = jnp.bfloat16


def _dot(a, b):
    return jnp.dot(a, b, preferred_element_type=_F32)


def _dot_nt(a, b):
    return lax.dot_general(a, b, (((1,), (1,)), ((), ())), preferred_element_type=_F32)


def _proj_kernel(x_ref, g_ref, w_ref, qkg_ref, o_ref):
    x = x_ref[...]
    ms = jnp.mean(x * x, axis=-1, keepdims=True)
    h = (x * lax.rsqrt(ms + NORM_EPS) * g_ref[...]).astype(_BF16)
    rows = x.shape[0]
    low_half = lax.broadcasted_iota(jnp.int32, (rows, LANES), 1) < HEAD_DIM
    q_scale = HEAD_DIM ** -0.5 * LOG2E
    norm_groups = {0: (0, q_scale), 1: (1, 1.0), 3: (2, q_scale), 4: (3, 1.0)}
    for grp in range(N_GROUPS):
        yg = _dot(h, w_ref[:, grp * GROUP_WIDTH:(grp + 1) * GROUP_WIDTH])
        for c in range(BLK_PER_GROUP):
            y = yg[:, c * LANES:(c + 1) * LANES]
            if grp in norm_groups:
                gain_row, scale = norm_groups[grp]
                sq = y * y
                tot = jnp.sum(sq, axis=-1, keepdims=True)
                lo = jnp.sum(jnp.where(low_half, sq, 0.0), axis=-1, keepdims=True)
                ms_h = jnp.where(low_half, lo, tot - lo) * (1.0 / HEAD_DIM)
                gain = qkg_ref[gain_row:gain_row + 1, :]
                y = y * lax.rsqrt(ms_h + NORM_EPS) * gain
                if scale != 1.0:
                    y = y * scale
            o_ref[grp * BLK_PER_GROUP + c] = y.astype(_BF16)


def _proj(x2d, g, w_stack, layer, qkg):
    n = x2d.shape[0]
    return pl.pallas_call(
        _proj_kernel,
        out_shape=jax.ShapeDtypeStruct((N_COLBLK, n, LANES), _BF16),
        grid=(n // PROJ_ROWS,),
        in_specs=[
            pl.BlockSpec((PROJ_ROWS, D_MODEL), lambda r: (r, 0)),
            pl.BlockSpec((1, D_MODEL), lambda r: (0, 0)),
            pl.BlockSpec((None, D_MODEL, N_GROUPS * GROUP_WIDTH), lambda r: (layer, 0, 0),
                         pipeline_mode=pl.Buffered(1)),
            pl.BlockSpec((4, LANES), lambda r: (0, 0)),
        ],
        out_specs=pl.BlockSpec((N_COLBLK, PROJ_ROWS, LANES), lambda r: (0, r, 0)),
        compiler_params=pltpu.CompilerParams(
            dimension_semantics=("arbitrary",), vmem_limit_bytes=VMEM_LIMIT),
        name="proj",
    )(x2d, g, w_stack, qkg)


def _split_halves(q):
    low_half = lax.broadcasted_iota(jnp.int32, q.shape, 1) < HEAD_DIM
    zero = jnp.zeros_like(q)
    return jnp.where(low_half, q, zero), jnp.where(low_half, zero, q)


def _stage_operands(q_ref, v_ref, qlo_ref, qhi_ref, vt_ref, feature_groups):
    q_lo, q_hi = _split_halves(q_ref[0])
    qlo_ref[...] = q_lo
    qhi_ref[...] = q_hi
    feat = LANES // feature_groups
    ones_rows = (lax.broadcasted_iota(jnp.int32, (ONES_ROWS, ATT_TILE), 0) == 0).astype(_BF16)
    for n in range(vt_ref.shape[0]):
        v_t = v_ref[0, n * ATT_TILE:(n + 1) * ATT_TILE, :].astype(_F32).T.astype(_BF16)
        for g in range(feature_groups):
            vt_ref[n, g, 0:feat, :] = v_t[g * feat:(g + 1) * feat]
            vt_ref[n, g, feat:feat + ONES_ROWS, :] = ones_rows


def _transposed_out_spec(nq):
    return pl.BlockSpec((1, nq, LANES, ATT_TILE), lambda h, b: (h, b, 0, 0))


def _causal_attention(nq, qlo_ref, qhi_ref, k_ref, vt_ref, bias_ref, bias_rows, sel_ref, mx_ref, acc_ref,
                      value_group, finish):
    t = ATT_TILE

    def logits(qi, jk, stream):
        q_x = (qlo_ref, qhi_ref)[stream][qi * t:(qi + 1) * t, :]
        k_t = k_ref[0, jk * t:(jk + 1) * t, :]
        return _dot_nt(k_t, q_x) + bias_ref[bias_rows[stream], qi - jk]

    items = [(qi, jk, stream) for qi in range(nq) for jk in [qi] + list(range(qi)) for stream in range(2)]
    ahead = [logits(*it) for it in items[:QK_AHEAD]]
    for n, (qi, jk, stream) in enumerate(items):
        s_cur = ahead.pop(0)
        if n + QK_AHEAD < len(items):
            ahead.append(logits(*items[n + QK_AHEAD]))
        masked = jk != qi and sel_ref is not None
        c = jnp.max(s_cur, axis=0, keepdims=True)
        if masked:
            sel_row = sel_ref[qi, stream, jk:jk + 1, :]
            c = c + sel_row
        if jk == qi:
            m_new = c
            m_row = m_new
        else:
            m_prev = mx_ref[qi, stream]
            m_new = jnp.maximum(m_prev, c)
            alpha = jnp.exp2(m_prev - m_new)
            m_row = m_new - sel_row if masked else m_new
        pexp = jnp.exp2(s_cur - m_row).astype(_BF16)
        pv = _dot(vt_ref[jk, value_group(stream)], pexp)
        acc_ref[qi, stream] = pv if jk == qi else alpha * acc_ref[qi, stream] + pv
        mx_ref[qi, stream] = m_new
        if stream == 1 and jk == max(qi - 1, 0) and (qi == 0 or jk != qi):
            finish(qi)


def _diff_kernel(lam_init_ref, lamp_ref, q_ref, k_ref, v_ref, bias_ref, subg_ref, o_ref,
                 qlo_ref, qhi_ref, vt_ref, mx_ref, acc_ref):
    nq = vt_ref.shape[0]
    _stage_operands(q_ref, v_ref, qlo_ref, qhi_ref, vt_ref, 1)

    lf = lamp_ref[...]
    lam_init = lam_init_ref[0]
    lam = (jnp.exp(jnp.sum(lf[0:1, :] * lf[1:2, :], axis=-1, keepdims=True))
           - jnp.exp(jnp.sum(lf[2:3, :] * lf[3:4, :], axis=-1, keepdims=True)) + lam_init)

    def finish(qi):
        a0, a1 = acc_ref[qi, 0], acc_ref[qi, 1]
        out_t = (a0[:LANES] / a0[LANES:LANES + 1]
                 - lam * (a1[:LANES] / a1[LANES:LANES + 1]))
        ms = jnp.mean(out_t * out_t, axis=0, keepdims=True)
        out_t = out_t * lax.rsqrt(ms + NORM_EPS) * subg_ref[...] * (1.0 - lam_init)
        o_ref[0, qi] = out_t.astype(_BF16)

    _causal_attention(nq, qlo_ref, qhi_ref, k_ref, vt_ref, bias_ref, (0, 0), None, mx_ref, acc_ref,
                      lambda stream: 0, finish)


def _diff_attention(qkv, bias_tiles, lam_init, lam_params, sub_g, batch, seq):
    nq = seq // ATT_TILE
    heads = BLK_PER_GROUP
    whole_seq = lambda col0: pl.BlockSpec((1, seq, LANES), lambda h, b: (col0 + h, b, 0))
    return pl.pallas_call(
        _diff_kernel,
        out_shape=jax.ShapeDtypeStruct((heads, batch * nq, LANES, ATT_TILE), _BF16),
        grid=(heads, batch),
        in_specs=[
            pl.BlockSpec(memory_space=pltpu.SMEM),
            pl.BlockSpec((4, HEAD_DIM), lambda h, b: (0, 0)),
            whole_seq(0),
            whole_seq(BLK_PER_GROUP),
            whole_seq(2 * BLK_PER_GROUP),
            pl.BlockSpec((1, nq, ATT_TILE, ATT_TILE), lambda h, b: (h, 0, 0, 0)),
            pl.BlockSpec((LANES, 1), lambda h, b: (0, 0)),
        ],
        out_specs=_transposed_out_spec(nq),
        scratch_shapes=[
            pltpu.VMEM((seq, LANES), _BF16),
            pltpu.VMEM((seq, LANES), _BF16),
            pltpu.VMEM((nq, 1, LANES + ONES_ROWS, ATT_TILE), _BF16),
            pltpu.VMEM((nq, 2, 1, ATT_TILE), _F32),
            pltpu.VMEM((nq, 2, LANES + ONES_ROWS, ATT_TILE), _F32),
        ],
        compiler_params=pltpu.CompilerParams(
            dimension_semantics=("arbitrary", "arbitrary"), vmem_limit_bytes=VMEM_LIMIT),
        name="diff_attn",
    )(lam_init, lam_params, qkv, qkv, qkv, bias_tiles, sub_g)


def _moba_kernel(q_ref, k_ref, v_ref, bias_ref, o_ref, qlo_ref, qhi_ref, vt_ref, mx_ref, acc_ref,
                 kmean_ref, sel_ref):
    t = ATT_TILE
    nq = vt_ref.shape[0]
    nb = k_ref.shape[1] // MOBA_BLOCK
    _stage_operands(q_ref, v_ref, qlo_ref, qhi_ref, vt_ref, 2)

    kmean_ref[...] = jnp.zeros(kmean_ref.shape, _F32)
    for n in range(nb):
        kb = k_ref[0, n * MOBA_BLOCK:(n + 1) * MOBA_BLOCK, :].astype(_F32)
        kmean_ref[n:n + 1, :] = jnp.sum(kb, axis=0, keepdims=True) * (1.0 / MOBA_BLOCK)

    kmean = kmean_ref[...]
    km_hi = kmean.astype(_BF16)
    km_lo = (kmean - km_hi.astype(_F32)).astype(_BF16)
    block = lax.broadcasted_iota(jnp.int32, (GATE_ROWS, t), 0)
    for qi in range(1, nq):
        for stream, q_src in enumerate((qlo_ref, qhi_ref)):
            q_half = q_src[qi * t:(qi + 1) * t, :]
            gate = _dot_nt(km_hi, q_half) + _dot_nt(km_lo, q_half)
            for n in range(qi):
                g_n = gate[n:n + 1, :]
                beats = ((gate > g_n) | ((gate == g_n) & (block < n))) & (block < qi)
                rank = jnp.sum(jnp.where(beats, 1.0, 0.0), axis=0, keepdims=True)
                keep = (rank < MOBA_TOPK) & (jnp.abs(g_n) < jnp.inf)
                sel_ref[qi, stream, n:n + 1, :] = jnp.where(keep, 0.0, NEG)

    def finish(qi):
        heads_t = [acc_ref[qi, s, :HEAD_DIM] / acc_ref[qi, s, HEAD_DIM:HEAD_DIM + 1]
                   for s in range(2)]
        o_ref[0, qi] = jnp.concatenate(heads_t, axis=0).astype(_BF16)

    _causal_attention(nq, qlo_ref, qhi_ref, k_ref, vt_ref, bias_ref, (0, 1), sel_ref, mx_ref, acc_ref,
                      lambda stream: stream, finish)


def _moba_attention(qkv, bias_tiles, batch, seq):
    nq = seq // ATT_TILE
    pairs = BLK_PER_GROUP
    whole_seq = lambda col0: pl.BlockSpec((1, seq, LANES), lambda h, b: (col0 + h, b, 0))
    return pl.pallas_call(
        _moba_kernel,
        out_shape=jax.ShapeDtypeStruct((pairs, batch * nq, LANES, ATT_TILE), _BF16),
        grid=(pairs, batch),
        in_specs=[
            whole_seq(3 * BLK_PER_GROUP),
            whole_seq(4 * BLK_PER_GROUP),
            whole_seq(5 * BLK_PER_GROUP),
            pl.BlockSpec((2, nq, ATT_TILE, ATT_TILE), lambda h, b: (2 + h, 0, 0, 0)),
        ],
        out_specs=_transposed_out_spec(nq),
        scratch_shapes=[
            pltpu.VMEM((seq, LANES), _BF16),
            pltpu.VMEM((seq, LANES), _BF16),
            pltpu.VMEM((nq, 2, HEAD_DIM + ONES_ROWS, ATT_TILE), _BF16),
            pltpu.VMEM((nq, 2, 1, ATT_TILE), _F32),
            pltpu.VMEM((nq, 2, HEAD_DIM + ONES_ROWS, ATT_TILE), _F32),
            pltpu.VMEM((GATE_ROWS, LANES), _F32),
            pltpu.VMEM((nq, 2, GATE_ROWS, ATT_TILE), _F32),
        ],
        compiler_params=pltpu.CompilerParams(
            dimension_semantics=("arbitrary", "arbitrary"), vmem_limit_bytes=VMEM_LIMIT),
        name="moba_attn",
    )(qkv, qkv, qkv, bias_tiles)


def _ffn_kernel(tiles_per_seq, x_ref, yd_ref, ym_ref, wout_ref, g_ref, wup_ref, cw_ref, cb_ref,
                wdn_ref, o_ref, h_ref, act_ref, carry_ref, ext_ref):
    rows = x_ref.shape[0]
    heads = yd_ref.shape[0]

    @pl.when(pl.program_id(0) % tiles_per_seq == 0)
    def _():
        carry_ref[...] = jnp.zeros(carry_ref.shape, _F32)

    y_t = jnp.concatenate(
        [jnp.concatenate([y_ref[h, j] for j in range(y_ref.shape[1])], axis=-1)
         for y_ref in (yd_ref, ym_ref) for h in range(heads)], axis=0)
    x1 = x_ref[...] + lax.dot_general(y_t, wout_ref[...], (((0,), (0,)), ((), ())),
                                      preferred_element_type=_F32)
    o_ref[...] = x1
    ms = jnp.mean(x1 * x1, axis=-1, keepdims=True)
    h_ref[...] = (x1 * lax.rsqrt(ms + NORM_EPS) * g_ref[...]).astype(_BF16)

    def conv(u, col0, slot):
        cols = slice(col0, col0 + FFN_CHUNK)
        ext_ref[slot, 0:HALO, :] = carry_ref[:, cols]
        ext_ref[slot, HALO:HALO + rows, :] = u
        carry_ref[:, cols] = u[rows - HALO:rows, :]
        w = cw_ref[:, cols]
        return (w[0:1, :] * ext_ref[slot, HALO - 2:HALO - 2 + rows, :]
                + w[1:2, :] * ext_ref[slot, HALO - 1:HALO - 1 + rows, :]
                + w[2:3, :] * u + cb_ref[:, cols])

    h = h_ref[...]
    n_chunks = D_FF // FFN_CHUNK
    head_cols = (n_chunks - 2) * FFN_CHUNK
    for c in range(n_chunks):
        col_g = c * FFN_CHUNK
        col_u = D_FF + c * FFN_CHUNK
        slot = 2 * (c % 2)
        gate_raw = _dot(h, wup_ref[:, col_g:col_g + FFN_CHUNK])
        up_raw = _dot(h, wup_ref[:, col_u:col_u + FFN_CHUNK])
        if c == n_chunks - 1:
            o_ref[...] += _dot(act_ref[:, :head_cols], wdn_ref[:head_cols, :])
        gate = conv(gate_raw, col_g, slot)
        up = conv(up_raw, col_u, slot + 1)
        act = gate / (1.0 + jnp.exp(-gate)) * up
        act_ref[:, col_g:col_g + FFN_CHUNK] = act.astype(_BF16)
    o_ref[...] += _dot(act_ref[:, head_cols:], wdn_ref[head_cols:, :])


def _ffn(x2d, yd, ym, w_out, g, w_up, conv_w, conv_b, w_down, layer, seq):
    n = x2d.shape[0]
    heads = yd.shape[0]
    resident = functools.partial(pl.BlockSpec, index_map=lambda r: (0, 0),
                                 pipeline_mode=pl.Buffered(1))
    layer_slab = lambda rows, cols: pl.BlockSpec((None, rows, cols), lambda r: (layer, 0, 0),
                                                 pipeline_mode=pl.Buffered(1))
    return pl.pallas_call(
        functools.partial(_ffn_kernel, seq // FFN_ROWS),
        out_shape=jax.ShapeDtypeStruct((n, D_MODEL), _F32),
        grid=(n // FFN_ROWS,),
        in_specs=[
            pl.BlockSpec((FFN_ROWS, D_MODEL), lambda r: (r, 0)),
            pl.BlockSpec((heads, FFN_ROWS // ATT_TILE, LANES, ATT_TILE), lambda r: (0, r, 0, 0)),
            pl.BlockSpec((heads, FFN_ROWS // ATT_TILE, LANES, ATT_TILE), lambda r: (0, r, 0, 0)),
            layer_slab(D_MODEL, D_MODEL),
            resident((1, D_MODEL)),
            layer_slab(D_MODEL, 2 * D_FF),
            resident((CONV_WIDTH, 2 * D_FF)),
            resident((1, 2 * D_FF)),
            layer_slab(D_FF, D_MODEL),
        ],
        out_specs=pl.BlockSpec((FFN_ROWS, D_MODEL), lambda r: (r, 0)),
        scratch_shapes=[
            pltpu.VMEM((FFN_ROWS, D_MODEL), _BF16),
            pltpu.VMEM((FFN_ROWS, D_FF), _BF16),
            pltpu.VMEM((HALO, 2 * D_FF), _F32),
            pltpu.VMEM((4, HALO + FFN_ROWS, FFN_CHUNK), _F32),
        ],
        compiler_params=pltpu.CompilerParams(
            dimension_semantics=("arbitrary",), vmem_limit_bytes=VMEM_LIMIT),
        name="ffn",
    )(x2d, yd, ym, w_out, g, w_up, conv_w, conv_b, w_down)


def _distance_bucket(dist):
    n = jnp.maximum(dist, 0)
    nf = jnp.maximum(n, REL_MAX_EXACT).astype(_F32)
    large = REL_MAX_EXACT + (jnp.log(nf / REL_MAX_EXACT) / math.log(REL_MAX_DISTANCE / REL_MAX_EXACT)
                             * (REL_BUCKETS - REL_MAX_EXACT)).astype(jnp.int32)
    large = jnp.minimum(large, REL_BUCKETS - 1)
    return jnp.where(n < REL_MAX_EXACT, n, large)


def _bias_tile_kernel(w_ref, o_ref):
    t = ATT_TILE
    for d in range(o_ref.shape[1]):
        rows = jnp.broadcast_to(w_ref[0, d:d + 1, :], (t, 2 * t))
        o_ref[0, d] = pltpu.roll(rows, 0, 1, stride=1, stride_axis=0)[:, :t]


def _bias_tiles(rel_bias, seq):
    t = ATT_TILE
    nt = seq // t
    by_dist = rel_bias[_distance_bucket(jnp.arange(seq))].T * LOG2E
    heads = by_dist.shape[0]
    ext = jnp.concatenate([jnp.full((heads, t), NEG, _F32), by_dist], axis=1)
    idx = (jnp.arange(nt) * t)[:, None] + jnp.arange(t)[None, :]
    w = jnp.concatenate([ext[:, t + idx], ext[:, idx]], axis=-1)
    return pl.pallas_call(
        _bias_tile_kernel,
        out_shape=jax.ShapeDtypeStruct((heads, nt, t, t), _F32),
        grid=(heads,),
        in_specs=[pl.BlockSpec((1, nt, 2 * t), lambda h: (h, 0, 0))],
        out_specs=pl.BlockSpec((1, nt, t, t), lambda h: (h, 0, 0, 0)),
        compiler_params=pltpu.CompilerParams(dimension_semantics=("arbitrary",)),
        name="bias_tiles",
    )(w)


def kernel(x, ln_attn_g, w_in, qk_norm_g, diff_lambda, diff_subln_g, w_out, ln_ffn_g, w_up,
           conv_w, conv_b, w_down, rel_bias):
    batch, seq, d_model = x.shape
    depth = w_in.shape[0]
    assert d_model == D_MODEL and seq % ATT_TILE == 0 and ATT_TILE == MOBA_BLOCK
    assert seq % FFN_ROWS == 0 and FFN_ROWS % ATT_TILE == 0 and (batch * seq) % PROJ_ROWS == 0

    bias_tiles = _bias_tiles(rel_bias.astype(_F32), seq)
    w_in_b, w_out_b = w_in.astype(_BF16), w_out.astype(_BF16)
    w_up_b, w_down_b = w_up.astype(_BF16), w_down.astype(_BF16)
    qk_gain = jnp.concatenate([qk_norm_g, qk_norm_g], axis=-1).astype(_F32)

    xs = x.reshape(batch * seq, d_model).astype(_F32)
    for i in range(depth):
        lam_init = jnp.full((1,), 0.8 - 0.6 * math.exp(-0.3 * i), _F32)
        qkv = _proj(xs, ln_attn_g[i][None, :], w_in_b, i, qk_gain[i])
        yd = _diff_attention(qkv, bias_tiles, lam_init, diff_lambda[i].astype(_F32),
                             diff_subln_g[i][:, None], batch, seq)
        ym = _moba_attention(qkv, bias_tiles, batch, seq)
        xs = _ffn(xs, yd, ym, w_out_b, ln_ffn_g[i][None, :], w_up_b, conv_w[i],
                  conv_b[i][None, :], w_down_b, i, seq)
    return xs.reshape(batch, seq, d_model).astype(x.dtype)
```

```python
import functools
import math

import jax
import jax.numpy as jnp
from jax import lax
from jax.experimental import pallas as pl
from jax.experimental.pallas import tpu as pltpu

D_MODEL = 1024
HEAD_DIM = 64
LANES = 128
N_GROUPS = 6
GROUP_WIDTH = 512
N_COLBLK = N_GROUPS * GROUP_WIDTH // LANES
BLK_PER_GROUP = GROUP_WIDTH // LANES
D_FF = 2816
CONV_WIDTH = 3
MOBA_BLOCK = 256
MOBA_TOPK = 3
REL_BUCKETS = 32
REL_MAX_EXACT = 16
REL_MAX_DISTANCE = 1024
NORM_EPS = 1e-6
NEG = -1e30

ATT_TILE = 256
GATE_ROWS = 16
ONES_ROWS = 16
LOG2E = math.log2(math.e)
QK_AHEAD = 4
PROJ_ROWS = 512
FFN_ROWS = 512
FFN_CHUNK = 256
HALO = 8
VMEM_LIMIT = 56 * 1024 * 1024

_F32 = jnp.float32
_BF16 = jnp.bfloat16


def _dot(a, b):
    return jnp.dot(a, b, preferred_element_type=_F32)


def _dot_nt(a, b):
    return lax.dot_general(a, b, (((1,), (1,)), ((), ())), preferred_element_type=_F32)


def _proj_kernel(x_ref, g_ref, w_ref, qkg_ref, o_ref):
    x = x_ref[...]
    ms = jnp.mean(x * x, axis=-1, keepdims=True)
    h = (x * lax.rsqrt(ms + NORM_EPS) * g_ref[...]).astype(_BF16)
    rows = x.shape[0]
    low_half = lax.broadcasted_iota(jnp.int32, (rows, LANES), 1) < HEAD_DIM
    q_scale = HEAD_DIM ** -0.5 * LOG2E
    norm_groups = {0: (0, q_scale), 1: (1, 1.0), 3: (2, q_scale), 4: (3, 1.0)}
    for grp in range(N_GROUPS):
        yg = _dot(h, w_ref[:, grp * GROUP_WIDTH:(grp + 1) * GROUP_WIDTH])
        for c in range(BLK_PER_GROUP):
            y = yg[:, c * LANES:(c + 1) * LANES]
            if grp in norm_groups:
                gain_row, scale = norm_groups[grp]
                sq = y * y
                tot = jnp.sum(sq, axis=-1, keepdims=True)
                lo = jnp.sum(jnp.where(low_half, sq, 0.0), axis=-1, keepdims=True)
                ms_h = jnp.where(low_half, lo, tot - lo) * (1.0 / HEAD_DIM)
                gain = qkg_ref[gain_row:gain_row + 1, :]
                y = y * lax.rsqrt(ms_h + NORM_EPS) * gain
                if scale != 1.0:
                    y = y * scale
            o_ref[grp * BLK_PER_GROUP + c] = y.astype(_BF16)


def _proj(x2d, g, w_stack, layer, qkg):
    n = x2d.shape[0]
    return pl.pallas_call(
        _proj_kernel,
        out_shape=jax.ShapeDtypeStruct((N_COLBLK, n, LANES), _BF16),
        grid=(n // PROJ_ROWS,),
        in_specs=[
            pl.BlockSpec((PROJ_ROWS, D_MODEL), lambda r: (r, 0)),
            pl.BlockSpec((1, D_MODEL), lambda r: (0, 0)),
            pl.BlockSpec((None, D_MODEL, N_GROUPS * GROUP_WIDTH), lambda r: (layer, 0, 0),
                         pipeline_mode=pl.Buffered(1)),
            pl.BlockSpec((4, LANES), lambda r: (0, 0)),
        ],
        out_specs=pl.BlockSpec((N_COLBLK, PROJ_ROWS, LANES), lambda r: (0, r, 0)),
        compiler_params=pltpu.CompilerParams(
            dimension_semantics=("arbitrary",), vmem_limit_bytes=VMEM_LIMIT),
        name="proj",
    )(x2d, g, w_stack, qkg)


def _split_halves(q):
    low_half = lax.broadcasted_iota(jnp.int32, q.shape, 1) < HEAD_DIM
    zero = jnp.zeros_like(q)
    return jnp.where(low_half, q, zero), jnp.where(low_half, zero, q)


def _stage_operands(q_ref, v_ref, qlo_ref, qhi_ref, vt_ref, feature_groups):
    q_lo, q_hi = _split_halves(q_ref[0])
    qlo_ref[...] = q_lo
    qhi_ref[...] = q_hi
    feat = LANES // feature_groups
    ones_rows = (lax.broadcasted_iota(jnp.int32, (ONES_ROWS, ATT_TILE), 0) == 0).astype(_BF16)
    for n in range(vt_ref.shape[0]):
        v_t = v_ref[0, n * ATT_TILE:(n + 1) * ATT_TILE, :].astype(_F32).T.astype(_BF16)
        for g in range(feature_groups):
            vt_ref[n, g, 0:feat, :] = v_t[g * feat:(g + 1) * feat]
            vt_ref[n, g, feat:feat + ONES_ROWS, :] = ones_rows


def _transposed_out_spec(nq):
    return pl.BlockSpec((1, nq, LANES, ATT_TILE), lambda h, b: (h, b, 0, 0))


def _causal_attention(nq, qlo_ref, qhi_ref, k_ref, vt_ref, bias_ref, bias_rows, sel_ref, mx_ref, acc_ref,
                      value_group, finish):
    t = ATT_TILE

    def logits(qi, jk, stream):
        q_x = (qlo_ref, qhi_ref)[stream][qi * t:(qi + 1) * t, :]
        k_t = k_ref[0, jk * t:(jk + 1) * t, :]
        return _dot_nt(k_t, q_x) + bias_ref[bias_rows[stream], qi - jk]

    items = [(qi, jk, stream) for qi in range(nq) for jk in [qi] + list(range(qi)) for stream in range(2)]
    ahead = [logits(*it) for it in items[:QK_AHEAD]]
    for n, (qi, jk, stream) in enumerate(items):
        s_cur = ahead.pop(0)
        if n + QK_AHEAD < len(items):
            ahead.append(logits(*items[n + QK_AHEAD]))
        masked = jk != qi and sel_ref is not None
        c = jnp.max(s_cur, axis=0, keepdims=True)
        if masked:
            sel_row = sel_ref[qi, stream, jk:jk + 1, :]
            c = c + sel_row
        if jk == qi:
            m_new = c
            m_row = m_new
        else:
            m_prev = mx_ref[qi, stream]
            m_new = jnp.maximum(m_prev, c)
            alpha = jnp.exp2(m_prev - m_new)
            m_row = m_new - sel_row if masked else m_new
        pexp = jnp.exp2(s_cur - m_row).astype(_BF16)
        pv = _dot(vt_ref[jk, value_group(stream)], pexp)
        acc_ref[qi, stream] = pv if jk == qi else alpha * acc_ref[qi, stream] + pv
        mx_ref[qi, stream] = m_new
        if stream == 1 and jk == max(qi - 1, 0) and (qi == 0 or jk != qi):
            finish(qi)


def _diff_kernel(lam_init_ref, lamp_ref, q_ref, k_ref, v_ref, bias_ref, subg_ref, o_ref,
                 qlo_ref, qhi_ref, vt_ref, mx_ref, acc_ref):
    nq = vt_ref.shape[0]
    _stage_operands(q_ref, v_ref, qlo_ref, qhi_ref, vt_ref, 1)

    lf = lamp_ref[...]
    lam_init = lam_init_ref[0]
    lam = (jnp.exp(jnp.sum(lf[0:1, :] * lf[1:2, :], axis=-1, keepdims=True))
           - jnp.exp(jnp.sum(lf[2:3, :] * lf[3:4, :], axis=-1, keepdims=True)) + lam_init)

    def finish(qi):
        a0, a1 = acc_ref[qi, 0], acc_ref[qi, 1]
        out_t = (a0[:LANES] / a0[LANES:LANES + 1]
                 - lam * (a1[:LANES] / a1[LANES:LANES + 1]))
        ms = jnp.mean(out_t * out_t, axis=0, keepdims=True)
        out_t = out_t * lax.rsqrt(ms + NORM_EPS) * subg_ref[...] * (1.0 - lam_init)
        o_ref[0, qi] = out_t.astype(_BF16)

    _causal_attention(nq, qlo_ref, qhi_ref, k_ref, vt_ref, bias_ref, (0, 0), None, mx_ref, acc_ref,
                      lambda stream: 0, finish)


def _diff_attention(qkv, bias_tiles, lam_init, lam_params, sub_g, batch, seq):
    nq = seq // ATT_TILE
    heads = BLK_PER_GROUP
    whole_seq = lambda col0: pl.BlockSpec((1, seq, LANES), lambda h, b: (col0 + h, b, 0))
    return pl.pallas_call(
        _diff_kernel,
        out_shape=jax.ShapeDtypeStruct((heads, batch * nq, LANES, ATT_TILE), _BF16),
        grid=(heads, batch),
        in_specs=[
            pl.BlockSpec(memory_space=pltpu.SMEM),
            pl.BlockSpec((4, HEAD_DIM), lambda h, b: (0, 0)),
            whole_seq(0),
            whole_seq(BLK_PER_GROUP),
            whole_seq(2 * BLK_PER_GROUP),
            pl.BlockSpec((1, nq, ATT_TILE, ATT_TILE), lambda h, b: (h, 0, 0, 0)),
            pl.BlockSpec((LANES, 1), lambda h, b: (0, 0)),
        ],
        out_specs=_transposed_out_spec(nq),
        scratch_shapes=[
            pltpu.VMEM((seq, LANES), _BF16),
            pltpu.VMEM((seq, LANES), _BF16),
            pltpu.VMEM((nq, 1, LANES + ONES_ROWS, ATT_TILE), _BF16),
            pltpu.VMEM((nq, 2, 1, ATT_TILE), _F32),
            pltpu.VMEM((nq, 2, LANES + ONES_ROWS, ATT_TILE), _F32),
        ],
        compiler_params=pltpu.CompilerParams(
            dimension_semantics=("arbitrary", "arbitrary"), vmem_limit_bytes=VMEM_LIMIT),
        name="diff_attn",
    )(lam_init, lam_params, qkv, qkv, qkv, bias_tiles, sub_g)


def _moba_kernel(q_ref, k_ref, v_ref, bias_ref, o_ref, qlo_ref, qhi_ref, vt_ref, mx_ref, acc_ref,
                 kmean_ref, sel_ref):
    t = ATT_TILE
    nq = vt_ref.shape[0]
    nb = k_ref.shape[1] // MOBA_BLOCK
    _stage_operands(q_ref, v_ref, qlo_ref, qhi_ref, vt_ref, 2)

    kmean_ref[...] = jnp.zeros(kmean_ref.shape, _F32)
    for n in range(nb):
        kb = k_ref[0, n * MOBA_BLOCK:(n + 1) * MOBA_BLOCK, :].astype(_F32)
        kmean_ref[n:n + 1, :] = jnp.sum(kb, axis=0, keepdims=True) * (1.0 / MOBA_BLOCK)

    kmean = kmean_ref[...]
    km_hi = kmean.astype(_BF16)
    km_lo = (kmean - km_hi.astype(_F32)).astype(_BF16)
    block = lax.broadcasted_iota(jnp.int32, (GATE_ROWS, t), 0)
    for qi in range(1, nq):
        for stream, q_src in enumerate((qlo_ref, qhi_ref)):
            q_half = q_src[qi * t:(qi + 1) * t, :]
            gate = _dot_nt(km_hi, q_half) + _dot_nt(km_lo, q_half)
            for n in range(qi):
                g_n = gate[n:n + 1, :]
                beats = ((gate > g_n) | ((gate == g_n) & (block < n))) & (block < qi)
                rank = jnp.sum(jnp.where(beats, 1.0, 0.0), axis=0, keepdims=True)
                keep = (rank < MOBA_TOPK) & (jnp.abs(g_n) < jnp.inf)
                sel_ref[qi, stream, n:n + 1, :] = jnp.where(keep, 0.0, NEG)

    def finish(qi):
        heads_t = [acc_ref[qi, s, :HEAD_DIM] / acc_ref[qi, s, HEAD_DIM:HEAD_DIM + 1]
                   for s in range(2)]
        o_ref[0, qi] = jnp.concatenate(heads_t, axis=0).astype(_BF16)

    _causal_attention(nq, qlo_ref, qhi_ref, k_ref, vt_ref, bias_ref, (0, 1), sel_ref, mx_ref, acc_ref,
                      lambda stream: stream, finish)


def _moba_attention(qkv, bias_tiles, batch, seq):
    nq = seq // ATT_TILE
    pairs = BLK_PER_GROUP
    whole_seq = lambda col0: pl.BlockSpec((1, seq, LANES), lambda h, b: (col0 + h, b, 0))
    return pl.pallas_call(
        _moba_kernel,
        out_shape=jax.ShapeDtypeStruct((pairs, batch * nq, LANES, ATT_TILE), _BF16),
        grid=(pairs, batch),
        in_specs=[
            whole_seq(3 * BLK_PER_GROUP),
            whole_seq(4 * BLK_PER_GROUP),
            whole_seq(5 * BLK_PER_GROUP),
            pl.BlockSpec((2, nq, ATT_TILE, ATT_TILE), lambda h, b: (2 + h, 0, 0, 0)),
        ],
        out_specs=_transposed_out_spec(nq),
        scratch_shapes=[
            pltpu.VMEM((seq, LANES), _BF16),
            pltpu.VMEM((seq, LANES), _BF16),
            pltpu.VMEM((nq, 2, HEAD_DIM + ONES_ROWS, ATT_TILE), _BF16),
            pltpu.VMEM((nq, 2, 1, ATT_TILE), _F32),
            pltpu.VMEM((nq, 2, HEAD_DIM + ONES_ROWS, ATT_TILE), _F32),
            pltpu.VMEM((GATE_ROWS, LANES), _F32),
            pltpu.VMEM((nq, 2, GATE_ROWS, ATT_TILE), _F32),
        ],
        compiler_params=pltpu.CompilerParams(
            dimension_semantics=("arbitrary", "arbitrary"), vmem_limit_bytes=VMEM_LIMIT),
        name="moba_attn",
    )(qkv, qkv, qkv, bias_tiles)


def _ffn_kernel(tiles_per_seq, x_ref, yd_ref, ym_ref, wout_ref, g_ref, wup_ref, cw_ref, cb_ref,
                wdn_ref, o_ref, h_ref, act_ref, carry_ref):
    rows = x_ref.shape[0]
    heads = yd_ref.shape[0]

    @pl.when(pl.program_id(0) % tiles_per_seq == 0)
    def _():
        carry_ref[...] = jnp.zeros(carry_ref.shape, _F32)

    y_t = jnp.concatenate(
        [jnp.concatenate([y_ref[h, j] for j in range(y_ref.shape[1])], axis=-1)
         for y_ref in (yd_ref, ym_ref) for h in range(heads)], axis=0)
    x1 = x_ref[...] + lax.dot_general(y_t, wout_ref[...], (((0,), (0,)), ((), ())),
                                      preferred_element_type=_F32)
    o_ref[...] = x1
    ms = jnp.mean(x1 * x1, axis=-1, keepdims=True)
    h_ref[...] = (x1 * lax.rsqrt(ms + NORM_EPS) * g_ref[...]).astype(_BF16)

    first_rows = lax.broadcasted_iota(jnp.int32, (HALO, FFN_CHUNK), 0)

    def shifted(u, prev, k):
        rolled = pltpu.roll(u, k, 0)
        head = jnp.where(first_rows < k, pltpu.roll(prev, k, 0), rolled[:HALO])
        return jnp.concatenate([head, rolled[HALO:]], axis=0)

    def conv(u, col0):
        cols = slice(col0, col0 + FFN_CHUNK)
        prev = carry_ref[:, cols]
        carry_ref[:, cols] = u[rows - HALO:rows, :]
        w = cw_ref[:, cols]
        return (w[0:1, :] * shifted(u, prev, 2) + w[1:2, :] * shifted(u, prev, 1)
                + w[2:3, :] * u + cb_ref[:, cols])

    h = h_ref[...]
    n_chunks = D_FF // FFN_CHUNK
    head_cols = (n_chunks - 2) * FFN_CHUNK
    for c in range(n_chunks):
        col_g = c * FFN_CHUNK
        col_u = D_FF + c * FFN_CHUNK
        gate_raw = _dot(h, wup_ref[:, col_g:col_g + FFN_CHUNK])
        up_raw = _dot(h, wup_ref[:, col_u:col_u + FFN_CHUNK])
        if c == n_chunks - 1:
            o_ref[...] += _dot(act_ref[:, :head_cols], wdn_ref[:head_cols, :])
        gate = conv(gate_raw, col_g)
        up = conv(up_raw, col_u)
        act = gate / (1.0 + jnp.exp(-gate)) * up
        act_ref[:, col_g:col_g + FFN_CHUNK] = act.astype(_BF16)
    o_ref[...] += _dot(act_ref[:, head_cols:], wdn_ref[head_cols:, :])


def _ffn(x2d, yd, ym, w_out, g, w_up, conv_w, conv_b, w_down, layer, seq):
    n = x2d.shape[0]
    heads = yd.shape[0]
    resident = functools.partial(pl.BlockSpec, index_map=lambda r: (0, 0),
                                 pipeline_mode=pl.Buffered(1))
    layer_slab = lambda rows, cols: pl.BlockSpec((None, rows, cols), lambda r: (layer, 0, 0),
                                                 pipeline_mode=pl.Buffered(1))
    return pl.pallas_call(
        functools.partial(_ffn_kernel, seq // FFN_ROWS),
        out_shape=jax.ShapeDtypeStruct((n, D_MODEL), _F32),
        grid=(n // FFN_ROWS,),
        in_specs=[
            pl.BlockSpec((FFN_ROWS, D_MODEL), lambda r: (r, 0)),
            pl.BlockSpec((heads, FFN_ROWS // ATT_TILE, LANES, ATT_TILE), lambda r: (0, r, 0, 0)),
            pl.BlockSpec((heads, FFN_ROWS // ATT_TILE, LANES, ATT_TILE), lambda r: (0, r, 0, 0)),
            layer_slab(D_MODEL, D_MODEL),
            resident((1, D_MODEL)),
            layer_slab(D_MODEL, 2 * D_FF),
            resident((CONV_WIDTH, 2 * D_FF)),
            resident((1, 2 * D_FF)),
            layer_slab(D_FF, D_MODEL),
        ],
        out_specs=pl.BlockSpec((FFN_ROWS, D_MODEL), lambda r: (r, 0)),
        scratch_shapes=[
            pltpu.VMEM((FFN_ROWS, D_MODEL), _BF16),
            pltpu.VMEM((FFN_ROWS, D_FF), _BF16),
            pltpu.VMEM((HALO, 2 * D_FF), _F32),
        ],
        compiler_params=pltpu.CompilerParams(
            dimension_semantics=("arbitrary",), vmem_limit_bytes=VMEM_LIMIT),
        name="ffn",
    )(x2d, yd, ym, w_out, g, w_up, conv_w, conv_b, w_down)


def _distance_bucket(dist):
    n = jnp.maximum(dist, 0)
    nf = jnp.maximum(n, REL_MAX_EXACT).astype(_F32)
    large = REL_MAX_EXACT + (jnp.log(nf / REL_MAX_EXACT) / math.log(REL_MAX_DISTANCE / REL_MAX_EXACT)
                             * (REL_BUCKETS - REL_MAX_EXACT)).astype(jnp.int32)
    large = jnp.minimum(large, REL_BUCKETS - 1)
    return jnp.where(n < REL_MAX_EXACT, n, large)


def _bias_tile_kernel(w_ref, o_ref):
    t = ATT_TILE
    for d in range(o_ref.shape[1]):
        rows = jnp.broadcast_to(w_ref[0, d:d + 1, :], (t, 2 * t))
        o_ref[0, d] = pltpu.roll(rows, 0, 1, stride=1, stride_axis=0)[:, :t]


def _bias_tiles(rel_bias, seq):
    t = ATT_TILE
    nt = seq // t
    by_dist = rel_bias[_distance_bucket(jnp.arange(seq))].T * LOG2E
    heads = by_dist.shape[0]
    ext = jnp.concatenate([jnp.full((heads, t), NEG, _F32), by_dist], axis=1)
    idx = (jnp.arange(nt) * t)[:, None] + jnp.arange(t)[None, :]
    w = jnp.concatenate([ext[:, t + idx], ext[:, idx]], axis=-1)
    return pl.pallas_call(
        _bias_tile_kernel,
        out_shape=jax.ShapeDtypeStruct((heads, nt, t, t), _F32),
        grid=(heads,),
        in_specs=[pl.BlockSpec((1, nt, 2 * t), lambda h: (h, 0, 0))],
        out_specs=pl.BlockSpec((1, nt, t, t), lambda h: (h, 0, 0, 0)),
        compiler_params=pltpu.CompilerParams(dimension_semantics=("arbitrary",)),
        name="bias_tiles",
    )(w)


def kernel(x, ln_attn_g, w_in, qk_norm_g, diff_lambda, diff_subln_g, w_out, ln_ffn_g, w_up,
           conv_w, conv_b, w_down, rel_bias):
    batch, seq, d_model = x.shape
    depth = w_in.shape[0]
    assert d_model == D_MODEL and seq % ATT_TILE == 0 and ATT_TILE == MOBA_BLOCK
    assert seq % FFN_ROWS == 0 and FFN_ROWS % ATT_TILE == 0 and (batch * seq) % PROJ_ROWS == 0

    bias_tiles = _bias_tiles(rel_bias.astype(_F32), seq)
    w_in_b, w_out_b = w_in.astype(_BF16), w_out.astype(_BF16)
    w_up_b, w_down_b = w_up.astype(_BF16), w_down.astype(_BF16)
    qk_gain = jnp.concatenate([qk_norm_g, qk_norm_g], axis=-1).astype(_F32)

    xs = x.reshape(batch * seq, d_model).astype(_F32)
    for i in range(depth):
        lam_init = jnp.full((1,), 0.8 - 0.6 * math.exp(-0.3 * i), _F32)
        qkv = _proj(xs, ln_attn_g[i][None, :], w_in_b, i, qk_gain[i])
        yd = _diff_attention(qkv, bias_tiles, lam_init, diff_lambda[i].astype(_F32),
                             diff_subln_g[i][:, None], batch, seq)
        ym = _moba_attention(qkv, bias_tiles, batch, seq)
        xs = _ffn(xs, yd, ym, w_out_b, ln_ffn_g[i][None, :], w_up_b, conv_w[i],
                  conv_b[i][None, :], w_down_b, i, seq)
    return xs.reshape(batch, seq, d_model).astype(x.dtype)
```

```python
import functools
import math

import jax
import jax.numpy as jnp
from jax import lax
from jax.experimental import pallas as pl
from jax.experimental.pallas import tpu as pltpu

D_MODEL = 1024
HEAD_DIM = 64
LANES = 128
N_GROUPS = 6
GROUP_WIDTH = 512
N_COLBLK = N_GROUPS * GROUP_WIDTH // LANES
BLK_PER_GROUP = GROUP_WIDTH // LANES
D_FF = 2816
CONV_WIDTH = 3
MOBA_BLOCK = 256
MOBA_TOPK = 3
REL_BUCKETS = 32
REL_MAX_EXACT = 16
REL_MAX_DISTANCE = 1024
NORM_EPS = 1e-6
NEG = -1e30

ATT_TILE = 256
GATE_ROWS = 16
ONES_ROWS = 16
LOG2E = math.log2(math.e)
QK_AHEAD = 4
PROJ_ROWS = 512
FFN_ROWS = 1024
FFN_CHUNK = 256
HALO = 8
VMEM_LIMIT = 56 * 1024 * 1024

_F32 = jnp.float32
_BF16 = jnp.bfloat16


def _dot(a, b):
    return jnp.dot(a, b, preferred_element_type=_F32)


def _dot_nt(a, b):
    return lax.dot_general(a, b, (((1,), (1,)), ((), ())), preferred_element_type=_F32)


def _proj_kernel(x_ref, g_ref, w_ref, qkg_ref, o_ref):
    x = x_ref[...]
    ms = jnp.mean(x * x, axis=-1, keepdims=True)
    h = (x * lax.rsqrt(ms + NORM_EPS) * g_ref[...]).astype(_BF16)
    rows = x.shape[0]
    low_half = lax.broadcasted_iota(jnp.int32, (rows, LANES), 1) < HEAD_DIM
    q_scale = HEAD_DIM ** -0.5 * LOG2E
    norm_groups = {0: (0, q_scale), 1: (1, 1.0), 3: (2, q_scale), 4: (3, 1.0)}
    for grp in range(N_GROUPS):
        yg = _dot(h, w_ref[:, grp * GROUP_WIDTH:(grp + 1) * GROUP_WIDTH])
        for c in range(BLK_PER_GROUP):
            y = yg[:, c * LANES:(c + 1) * LANES]
            if grp in norm_groups:
                gain_row, scale = norm_groups[grp]
                sq = y * y
                tot = jnp.sum(sq, axis=-1, keepdims=True)
                lo = jnp.sum(jnp.where(low_half, sq, 0.0), axis=-1, keepdims=True)
                ms_h = jnp.where(low_half, lo, tot - lo) * (1.0 / HEAD_DIM)
                gain = qkg_ref[gain_row:gain_row + 1, :]
                y = y * lax.rsqrt(ms_h + NORM_EPS) * gain
                if scale != 1.0:
                    y = y * scale
            o_ref[grp * BLK_PER_GROUP + c] = y.astype(_BF16)


def _proj(x2d, g, w_stack, layer, qkg):
    n = x2d.shape[0]
    return pl.pallas_call(
        _proj_kernel,
        out_shape=jax.ShapeDtypeStruct((N_COLBLK, n, LANES), _BF16),
        grid=(n // PROJ_ROWS,),
        in_specs=[
            pl.BlockSpec((PROJ_ROWS, D_MODEL), lambda r: (r, 0)),
            pl.BlockSpec((1, D_MODEL), lambda r: (0, 0)),
            pl.BlockSpec((None, D_MODEL, N_GROUPS * GROUP_WIDTH), lambda r: (layer, 0, 0),
                         pipeline_mode=pl.Buffered(1)),
            pl.BlockSpec((4, LANES), lambda r: (0, 0)),
        ],
        out_specs=pl.BlockSpec((N_COLBLK, PROJ_ROWS, LANES), lambda r: (0, r, 0)),
        compiler_params=pltpu.CompilerParams(
            dimension_semantics=("arbitrary",), vmem_limit_bytes=VMEM_LIMIT),
        name="proj",
    )(x2d, g, w_stack, qkg)


def _split_halves(q):
    low_half = lax.broadcasted_iota(jnp.int32, q.shape, 1) < HEAD_DIM
    zero = jnp.zeros_like(q)
    return jnp.where(low_half, q, zero), jnp.where(low_half, zero, q)


def _stage_queries(q_ref, qlo_ref, qhi_ref):
    q_lo, q_hi = _split_halves(q_ref[0])
    qlo_ref[...] = q_lo
    qhi_ref[...] = q_hi


def _stage_values(v_ref, vt_ref, n):
    feature_groups = vt_ref.shape[1]
    feat = LANES // feature_groups
    ones_rows = (lax.broadcasted_iota(jnp.int32, (ONES_ROWS, ATT_TILE), 0) == 0).astype(_BF16)
    v_t = v_ref[0, n * ATT_TILE:(n + 1) * ATT_TILE, :].astype(_F32).T.astype(_BF16)
    for g in range(feature_groups):
        vt_ref[n, g, 0:feat, :] = v_t[g * feat:(g + 1) * feat]
        vt_ref[n, g, feat:feat + ONES_ROWS, :] = ones_rows


def _bias_is_uniform(offset):
    return (offset - 1) * ATT_TILE + 1 >= REL_MAX_DISTANCE


def _transposed_out_spec(nq):
    return pl.BlockSpec((1, nq, LANES, ATT_TILE), lambda h, b: (h, b, 0, 0))


def _causal_attention(nq, qlo_ref, qhi_ref, k_ref, vt_ref, bias_ref, bias_rows, sel_ref, mx_ref, acc_ref,
                      value_group, prepare, finish, gate_lhs=None, select=None):
    t = ATT_TILE

    def logits(qi, jk, stream):
        q_x = (qlo_ref, qhi_ref)[stream][qi * t:(qi + 1) * t, :]
        k_t = k_ref[0, jk * t:(jk + 1) * t, :]
        if _bias_is_uniform(qi - jk):
            return _dot_nt(k_t, q_x), None
        bias = bias_ref[bias_rows[stream], qi - jk]
        if jk == qi and qi > 0 and gate_lhs is not None:
            s_ext = _dot_nt(jnp.concatenate([k_t, gate_lhs], axis=0), q_x)
            return s_ext[:t] + bias, s_ext[t:]
        return _dot_nt(k_t, q_x) + bias, None

    def row_shift(qi, jk, stream):
        terms = []
        if jk != qi and sel_ref is not None:
            terms.append(sel_ref[qi, stream, jk:jk + 1, :])
        if _bias_is_uniform(qi - jk):
            terms.append(bias_ref[bias_rows[stream], qi - jk, 0:1, :])
        return sum(terms[1:], terms[0]) if terms else None

    items = [(qi, jk, stream) for qi in range(nq) for jk in [qi] + list(range(qi)) for stream in range(2)]
    ahead = [logits(*it) for it in items[:QK_AHEAD]]
    for n, (qi, jk, stream) in enumerate(items):
        s_cur, gate = ahead.pop(0)
        if gate is not None:
            select(qi, stream, gate)
        if n + QK_AHEAD < len(items):
            ahead.append(logits(*items[n + QK_AHEAD]))
        if jk == qi and stream == 0:
            prepare(qi)
        shift = row_shift(qi, jk, stream)
        c = jnp.max(s_cur, axis=0, keepdims=True)
        if shift is not None:
            c = c + shift
        if jk == qi:
            m_new = c
        else:
            m_prev = mx_ref[qi, stream]
            m_new = jnp.maximum(m_prev, c)
            alpha = jnp.exp2(m_prev - m_new)
        m_row = m_new if shift is None else m_new - shift
        pexp = jnp.exp2(s_cur - m_row).astype(_BF16)
        pv = _dot(vt_ref[jk, value_group(stream)], pexp)
        acc_ref[qi, stream] = pv if jk == qi else alpha * acc_ref[qi, stream] + pv
        mx_ref[qi, stream] = m_new
        if stream == 1 and jk == max(qi - 1, 0) and (qi == 0 or jk != qi):
            finish(qi)


def _diff_kernel(lam_init_ref, lamp_ref, q_ref, k_ref, v_ref, bias_ref, subg_ref, o_ref,
                 qlo_ref, qhi_ref, vt_ref, mx_ref, acc_ref):
    nq = vt_ref.shape[0]
    _stage_queries(q_ref, qlo_ref, qhi_ref)

    lf = lamp_ref[...]
    lam_init = lam_init_ref[0]
    lam = (jnp.exp(jnp.sum(lf[0:1, :] * lf[1:2, :], axis=-1, keepdims=True))
           - jnp.exp(jnp.sum(lf[2:3, :] * lf[3:4, :], axis=-1, keepdims=True)) + lam_init)

    def finish(qi):
        a0, a1 = acc_ref[qi, 0], acc_ref[qi, 1]
        out_t = (a0[:LANES] / a0[LANES:LANES + 1]
                 - lam * (a1[:LANES] / a1[LANES:LANES + 1]))
        ms = jnp.mean(out_t * out_t, axis=0, keepdims=True)
        out_t = out_t * lax.rsqrt(ms + NORM_EPS) * subg_ref[...] * (1.0 - lam_init)
        o_ref[0, qi] = out_t.astype(_BF16)

    _causal_attention(nq, qlo_ref, qhi_ref, k_ref, vt_ref, bias_ref, (0, 0), None, mx_ref, acc_ref,
                      lambda stream: 0, lambda qi: _stage_values(v_ref, vt_ref, qi), finish)


def _diff_attention(qkv, bias_tiles, lam_init, lam_params, sub_g, batch, seq):
    nq = seq // ATT_TILE
    heads = BLK_PER_GROUP
    whole_seq = lambda col0: pl.BlockSpec((1, seq, LANES), lambda h, b: (col0 + h, b, 0))
    return pl.pallas_call(
        _diff_kernel,
        out_shape=jax.ShapeDtypeStruct((heads, batch * nq, LANES, ATT_TILE), _BF16),
        grid=(heads, batch),
        in_specs=[
            pl.BlockSpec(memory_space=pltpu.SMEM),
            pl.BlockSpec((4, HEAD_DIM), lambda h, b: (0, 0)),
            whole_seq(0),
            whole_seq(BLK_PER_GROUP),
            whole_seq(2 * BLK_PER_GROUP),
            pl.BlockSpec((1, nq, ATT_TILE, ATT_TILE), lambda h, b: (h, 0, 0, 0)),
            pl.BlockSpec((LANES, 1), lambda h, b: (0, 0)),
        ],
        out_specs=_transposed_out_spec(nq),
        scratch_shapes=[
            pltpu.VMEM((seq, LANES), _BF16),
            pltpu.VMEM((seq, LANES), _BF16),
            pltpu.VMEM((nq, 1, LANES + ONES_ROWS, ATT_TILE), _BF16),
            pltpu.VMEM((nq, 2, 1, ATT_TILE), _F32),
            pltpu.VMEM((nq, 2, LANES + ONES_ROWS, ATT_TILE), _F32),
        ],
        compiler_params=pltpu.CompilerParams(
            dimension_semantics=("arbitrary", "arbitrary"), vmem_limit_bytes=VMEM_LIMIT),
        name="diff_attn",
    )(lam_init, lam_params, qkv, qkv, qkv, bias_tiles, sub_g)


def _moba_kernel(q_ref, k_ref, v_ref, bias_ref, o_ref, qlo_ref, qhi_ref, vt_ref, mx_ref, acc_ref,
                 kmean_ref, sel_ref):
    t = ATT_TILE
    nq = vt_ref.shape[0]
    nb = k_ref.shape[1] // MOBA_BLOCK
    _stage_queries(q_ref, qlo_ref, qhi_ref)

    kmean_ref[...] = jnp.zeros(kmean_ref.shape, _F32)
    for n in range(nb):
        kb = k_ref[0, n * MOBA_BLOCK:(n + 1) * MOBA_BLOCK, :].astype(_F32)
        kmean_ref[n:n + 1, :] = jnp.sum(kb, axis=0, keepdims=True) * (1.0 / MOBA_BLOCK)

    kmean = kmean_ref[...]
    km_hi = kmean.astype(_BF16)
    km_lo = (kmean - km_hi.astype(_F32)).astype(_BF16)
    block = lax.broadcasted_iota(jnp.int32, (GATE_ROWS, t), 0)

    def select(qi, stream, gate_parts):
        gate = gate_parts[:GATE_ROWS] + gate_parts[GATE_ROWS:]
        for n in range(qi):
            g_n = gate[n:n + 1, :]
            beats = ((gate > g_n) | ((gate == g_n) & (block < n))) & (block < qi)
            rank = jnp.sum(jnp.where(beats, 1.0, 0.0), axis=0, keepdims=True)
            keep = (rank < MOBA_TOPK) & (jnp.abs(g_n) < jnp.inf)
            sel_ref[qi, stream, n:n + 1, :] = jnp.where(keep, 0.0, NEG)

    def finish(qi):
        heads_t = [acc_ref[qi, s, :HEAD_DIM] / acc_ref[qi, s, HEAD_DIM:HEAD_DIM + 1]
                   for s in range(2)]
        o_ref[0, qi] = jnp.concatenate(heads_t, axis=0).astype(_BF16)

    _causal_attention(nq, qlo_ref, qhi_ref, k_ref, vt_ref, bias_ref, (0, 1), sel_ref, mx_ref, acc_ref,
                      lambda stream: stream, lambda qi: _stage_values(v_ref, vt_ref, qi), finish,
                      gate_lhs=jnp.concatenate([km_hi, km_lo], axis=0), select=select)


def _moba_attention(qkv, bias_tiles, batch, seq):
    nq = seq // ATT_TILE
    pairs = BLK_PER_GROUP
    whole_seq = lambda col0: pl.BlockSpec((1, seq, LANES), lambda h, b: (col0 + h, b, 0))
    return pl.pallas_call(
        _moba_kernel,
        out_shape=jax.ShapeDtypeStruct((pairs, batch * nq, LANES, ATT_TILE), _BF16),
        grid=(pairs, batch),
        in_specs=[
            whole_seq(3 * BLK_PER_GROUP),
            whole_seq(4 * BLK_PER_GROUP),
            whole_seq(5 * BLK_PER_GROUP),
            pl.BlockSpec((2, nq, ATT_TILE, ATT_TILE), lambda h, b: (2 + h, 0, 0, 0)),
        ],
        out_specs=_transposed_out_spec(nq),
        scratch_shapes=[
            pltpu.VMEM((seq, LANES), _BF16),
            pltpu.VMEM((seq, LANES), _BF16),
            pltpu.VMEM((nq, 2, HEAD_DIM + ONES_ROWS, ATT_TILE), _BF16),
            pltpu.VMEM((nq, 2, 1, ATT_TILE), _F32),
            pltpu.VMEM((nq, 2, HEAD_DIM + ONES_ROWS, ATT_TILE), _F32),
            pltpu.VMEM((GATE_ROWS, LANES), _F32),
            pltpu.VMEM((nq, 2, GATE_ROWS, ATT_TILE), _F32),
        ],
        compiler_params=pltpu.CompilerParams(
            dimension_semantics=("arbitrary", "arbitrary"), vmem_limit_bytes=VMEM_LIMIT),
        name="moba_attn",
    )(qkv, qkv, qkv, bias_tiles)


def _ffn_kernel(tiles_per_seq, x_ref, yd_ref, ym_ref, wout_ref, g_ref, wup_ref, cw_ref, cb_ref,
                wdn_ref, o_ref, h_ref, act_ref, carry_ref):
    rows = x_ref.shape[0]
    heads = yd_ref.shape[0]

    @pl.when(pl.program_id(0) % tiles_per_seq == 0)
    def _():
        carry_ref[...] = jnp.zeros(carry_ref.shape, _F32)

    y_t = jnp.concatenate(
        [jnp.concatenate([y_ref[h, j] for j in range(y_ref.shape[1])], axis=-1)
         for y_ref in (yd_ref, ym_ref) for h in range(heads)], axis=0)
    x1 = x_ref[...] + lax.dot_general(y_t, wout_ref[...], (((0,), (0,)), ((), ())),
                                      preferred_element_type=_F32)
    o_ref[...] = x1
    ms = jnp.mean(x1 * x1, axis=-1, keepdims=True)
    h_ref[...] = (x1 * lax.rsqrt(ms + NORM_EPS) * g_ref[...]).astype(_BF16)

    first_rows = lax.broadcasted_iota(jnp.int32, (HALO, FFN_CHUNK), 0)

    def shifted(u, prev, k):
        rolled = pltpu.roll(u, k, 0)
        head = jnp.where(first_rows < k, pltpu.roll(prev, k, 0), rolled[:HALO])
        return jnp.concatenate([head, rolled[HALO:]], axis=0)

    def conv(u, col0):
        cols = slice(col0, col0 + FFN_CHUNK)
        prev = carry_ref[:, cols]
        carry_ref[:, cols] = u[rows - HALO:rows, :]
        w = cw_ref[:, cols]
        return (w[0:1, :] * shifted(u, prev, 2) + w[1:2, :] * shifted(u, prev, 1)
                + w[2:3, :] * u + cb_ref[:, cols])

    h = h_ref[...]
    n_chunks = D_FF // FFN_CHUNK
    head_cols = (n_chunks - 2) * FFN_CHUNK
    for c in range(n_chunks):
        col_g = c * FFN_CHUNK
        col_u = D_FF + c * FFN_CHUNK
        gate_raw = _dot(h, wup_ref[:, col_g:col_g + FFN_CHUNK])
        up_raw = _dot(h, wup_ref[:, col_u:col_u + FFN_CHUNK])
        if c == n_chunks - 1:
            o_ref[...] += _dot(act_ref[:, :head_cols], wdn_ref[:head_cols, :])
        gate = conv(gate_raw, col_g)
        up = conv(up_raw, col_u)
        act = gate / (1.0 + jnp.exp(-gate)) * up
        act_ref[:, col_g:col_g + FFN_CHUNK] = act.astype(_BF16)
    o_ref[...] += _dot(act_ref[:, head_cols:], wdn_ref[head_cols:, :])


def _ffn(x2d, yd, ym, w_out, g, w_up, conv_w, conv_b, w_down, layer, seq):
    n = x2d.shape[0]
    heads = yd.shape[0]
    resident = functools.partial(pl.BlockSpec, index_map=lambda r: (0, 0),
                                 pipeline_mode=pl.Buffered(1))
    layer_slab = lambda rows, cols: pl.BlockSpec((None, rows, cols), lambda r: (layer, 0, 0),
                                                 pipeline_mode=pl.Buffered(1))
    return pl.pallas_call(
        functools.partial(_ffn_kernel, seq // FFN_ROWS),
        out_shape=jax.ShapeDtypeStruct((n, D_MODEL), _F32),
        grid=(n // FFN_ROWS,),
        in_specs=[
            pl.BlockSpec((FFN_ROWS, D_MODEL), lambda r: (r, 0)),
            pl.BlockSpec((heads, FFN_ROWS // ATT_TILE, LANES, ATT_TILE), lambda r: (0, r, 0, 0)),
            pl.BlockSpec((heads, FFN_ROWS // ATT_TILE, LANES, ATT_TILE), lambda r: (0, r, 0, 0)),
            layer_slab(D_MODEL, D_MODEL),
            resident((1, D_MODEL)),
            layer_slab(D_MODEL, 2 * D_FF),
            resident((CONV_WIDTH, 2 * D_FF)),
            resident((1, 2 * D_FF)),
            layer_slab(D_FF, D_MODEL),
        ],
        out_specs=pl.BlockSpec((FFN_ROWS, D_MODEL), lambda r: (r, 0)),
        scratch_shapes=[
            pltpu.VMEM((FFN_ROWS, D_MODEL), _BF16),
            pltpu.VMEM((FFN_ROWS, D_FF), _BF16),
            pltpu.VMEM((HALO, 2 * D_FF), _F32),
        ],
        compiler_params=pltpu.CompilerParams(
            dimension_semantics=("arbitrary",), vmem_limit_bytes=VMEM_LIMIT),
        name="ffn",
    )(x2d, yd, ym, w_out, g, w_up, conv_w, conv_b, w_down)


def _distance_bucket(dist):
    n = jnp.maximum(dist, 0)
    nf = jnp.maximum(n, REL_MAX_EXACT).astype(_F32)
    large = REL_MAX_EXACT + (jnp.log(nf / REL_MAX_EXACT) / math.log(REL_MAX_DISTANCE / REL_MAX_EXACT)
                             * (REL_BUCKETS - REL_MAX_EXACT)).astype(jnp.int32)
    large = jnp.minimum(large, REL_BUCKETS - 1)
    return jnp.where(n < REL_MAX_EXACT, n, large)


def _bias_tile_kernel(w_ref, o_ref):
    t = ATT_TILE
    for d in range(o_ref.shape[1]):
        rows = jnp.broadcast_to(w_ref[0, d:d + 1, :], (t, 2 * t))
        o_ref[0, d] = pltpu.roll(rows, 0, 1, stride=1, stride_axis=0)[:, :t]


def _bias_tiles(rel_bias, seq):
    t = ATT_TILE
    nt = seq // t
    by_dist = rel_bias[_distance_bucket(jnp.arange(seq))].T * LOG2E
    heads = by_dist.shape[0]
    ext = jnp.concatenate([jnp.full((heads, t), NEG, _F32), by_dist], axis=1)
    idx = (jnp.arange(nt) * t)[:, None] + jnp.arange(t)[None, :]
    w = jnp.concatenate([ext[:, t + idx], ext[:, idx]], axis=-1)
    return pl.pallas_call(
        _bias_tile_kernel,
        out_shape=jax.ShapeDtypeStruct((heads, nt, t, t), _F32),
        grid=(heads,),
        in_specs=[pl.BlockSpec((1, nt, 2 * t), lambda h: (h, 0, 0))],
        out_specs=pl.BlockSpec((1, nt, t, t), lambda h: (h, 0, 0, 0)),
        compiler_params=pltpu.CompilerParams(dimension_semantics=("arbitrary",)),
        name="bias_tiles",
    )(w)


def kernel(x, ln_attn_g, w_in, qk_norm_g, diff_lambda, diff_subln_g, w_out, ln_ffn_g, w_up,
           conv_w, conv_b, w_down, rel_bias):
    batch, seq, d_model = x.shape
    depth = w_in.shape[0]
    assert d_model == D_MODEL and seq % ATT_TILE == 0 and ATT_TILE == MOBA_BLOCK
    assert seq % FFN_ROWS == 0 and FFN_ROWS % ATT_TILE == 0 and (batch * seq) % PROJ_ROWS == 0

    bias_tiles = _bias_tiles(rel_bias.astype(_F32), seq)
    w_in_b, w_out_b = w_in.astype(_BF16), w_out.astype(_BF16)
    w_up_b, w_down_b = w_up.astype(_BF16), w_down.astype(_BF16)
    qk_gain = jnp.concatenate([qk_norm_g, qk_norm_g], axis=-1).astype(_F32)

    xs = x.reshape(batch * seq, d_model).astype(_F32)
    for i in range(depth):
        lam_init = jnp.full((1,), 0.8 - 0.6 * math.exp(-0.3 * i), _F32)
        qkv = _proj(xs, ln_attn_g[i][None, :], w_in_b, i, qk_gain[i])
        yd = _diff_attention(qkv, bias_tiles, lam_init, diff_lambda[i].astype(_F32),
                             diff_subln_g[i][:, None], batch, seq)
        ym = _moba_attention(qkv, bias_tiles, batch, seq)
        xs = _ffn(xs, yd, ym, w_out_b, ln_ffn_g[i][None, :], w_up_b, conv_w[i],
                  conv_b[i][None, :], w_down_b, i, seq)
    return xs.reshape(batch, seq, d_model).astype(x.dtype)
```

```python
import functools
import math

import jax
import jax.numpy as jnp
from jax import lax
from jax.experimental import pallas as pl
from jax.experimental.pallas import tpu as pltpu

D_MODEL = 1024
HEAD_DIM = 64
LANES = 128
N_GROUPS = 6
GROUP_WIDTH = 512
N_COLBLK = N_GROUPS * GROUP_WIDTH // LANES
BLK_PER_GROUP = GROUP_WIDTH // LANES
D_FF = 2816
CONV_WIDTH = 3
MOBA_BLOCK = 256
MOBA_TOPK = 3
REL_BUCKETS = 32
REL_MAX_EXACT = 16
REL_MAX_DISTANCE = 1024
NORM_EPS = 1e-6
NEG = -1e30

ATT_TILE = 256
GATE_ROWS = 16
ONES_ROWS = 16
LOG2E = math.log2(math.e)
DIFF_QK_AHEAD = 4
MOBA_QK_AHEAD = 5
PROJ_ROWS = 1024
FFN_ROWS = 1024
FFN_CHUNK = 256
HALO = 8
VMEM_LIMIT = 56 * 1024 * 1024

_F32 = jnp.float32
_BF16 = jnp.bfloat16


def _dot(a, b):
    return jnp.dot(a, b, preferred_element_type=_F32)


def _dot_nt(a, b):
    return lax.dot_general(a, b, (((1,), (1,)), ((), ())), preferred_element_type=_F32)


def _proj_kernel(x_ref, g_ref, w_ref, qkg_ref, o_ref):
    x = x_ref[...]
    ms = jnp.mean(x * x, axis=-1, keepdims=True)
    h = (x * lax.rsqrt(ms + NORM_EPS) * g_ref[...]).astype(_BF16)
    rows = x.shape[0]
    low_half = lax.broadcasted_iota(jnp.int32, (rows, LANES), 1) < HEAD_DIM
    q_scale = HEAD_DIM ** -0.5 * LOG2E
    norm_groups = {0: (0, q_scale), 1: (1, 1.0), 3: (2, q_scale), 4: (3, 1.0)}
    for grp in range(N_GROUPS):
        yg = _dot(h, w_ref[:, grp * GROUP_WIDTH:(grp + 1) * GROUP_WIDTH])
        for c in range(BLK_PER_GROUP):
            y = yg[:, c * LANES:(c + 1) * LANES]
            if grp in norm_groups:
                gain_row, scale = norm_groups[grp]
                sq = y * y
                tot = jnp.sum(sq, axis=-1, keepdims=True)
                lo = jnp.sum(jnp.where(low_half, sq, 0.0), axis=-1, keepdims=True)
                ms_h = jnp.where(low_half, lo, tot - lo) * (1.0 / HEAD_DIM)
                gain = qkg_ref[gain_row:gain_row + 1, :]
                y = y * lax.rsqrt(ms_h + NORM_EPS) * gain
                if scale != 1.0:
                    y = y * scale
            o_ref[grp * BLK_PER_GROUP + c] = y.astype(_BF16)


def _proj(x2d, g, w_stack, layer, qkg):
    n = x2d.shape[0]
    return pl.pallas_call(
        _proj_kernel,
        out_shape=jax.ShapeDtypeStruct((N_COLBLK, n, LANES), _BF16),
        grid=(n // PROJ_ROWS,),
        in_specs=[
            pl.BlockSpec((PROJ_ROWS, D_MODEL), lambda r: (r, 0)),
            pl.BlockSpec((1, D_MODEL), lambda r: (0, 0)),
            pl.BlockSpec((None, D_MODEL, N_GROUPS * GROUP_WIDTH), lambda r: (layer, 0, 0),
                         pipeline_mode=pl.Buffered(1)),
            pl.BlockSpec((4, LANES), lambda r: (0, 0)),
        ],
        out_specs=pl.BlockSpec((N_COLBLK, PROJ_ROWS, LANES), lambda r: (0, r, 0)),
        compiler_params=pltpu.CompilerParams(
            dimension_semantics=("arbitrary",), vmem_limit_bytes=VMEM_LIMIT),
        name="proj",
    )(x2d, g, w_stack, qkg)


def _split_halves(q):
    low_half = lax.broadcasted_iota(jnp.int32, q.shape, 1) < HEAD_DIM
    zero = jnp.zeros_like(q)
    return jnp.where(low_half, q, zero), jnp.where(low_half, zero, q)


def _stage_queries(q_ref, qlo_ref, qhi_ref):
    q_lo, q_hi = _split_halves(q_ref[0])
    qlo_ref[...] = q_lo
    qhi_ref[...] = q_hi


def _stage_values(v_ref, vt_ref, n):
    feature_groups = vt_ref.shape[1]
    feat = LANES // feature_groups
    ones_rows = (lax.broadcasted_iota(jnp.int32, (ONES_ROWS, ATT_TILE), 0) == 0).astype(_BF16)
    v_t = v_ref[0, n * ATT_TILE:(n + 1) * ATT_TILE, :].astype(_F32).T.astype(_BF16)
    for g in range(feature_groups):
        vt_ref[n, g, 0:feat, :] = v_t[g * feat:(g + 1) * feat]
        vt_ref[n, g, feat:feat + ONES_ROWS, :] = ones_rows


def _bias_is_uniform(offset):
    return (offset - 1) * ATT_TILE + 1 >= REL_MAX_DISTANCE


def _transposed_out_spec(nq):
    return pl.BlockSpec((1, nq, LANES, ATT_TILE), lambda h, b: (h, b, 0, 0))


def _causal_attention(nq, qlo_ref, qhi_ref, k_ref, vt_ref, bias_ref, bias_rows, sel_ref, mx_ref, acc_ref,
                      value_group, prepare, finish, qk_ahead, gate_lhs=None, select=None):
    t = ATT_TILE

    def logits(qi, jk, stream):
        q_x = (qlo_ref, qhi_ref)[stream][qi * t:(qi + 1) * t, :]
        k_t = k_ref[0, jk * t:(jk + 1) * t, :]
        if _bias_is_uniform(qi - jk):
            return _dot_nt(k_t, q_x), None
        bias = bias_ref[bias_rows[stream], qi - jk]
        if jk == qi and qi > 0 and gate_lhs is not None:
            s_ext = _dot_nt(jnp.concatenate([k_t, gate_lhs], axis=0), q_x)
            return s_ext[:t] + bias, s_ext[t:]
        return _dot_nt(k_t, q_x) + bias, None

    def row_shift(qi, jk, stream):
        terms = []
        if jk != qi and sel_ref is not None:
            terms.append(sel_ref[qi, stream, jk:jk + 1, :])
        if _bias_is_uniform(qi - jk):
            terms.append(bias_ref[bias_rows[stream], qi - jk, 0:1, :])
        return sum(terms[1:], terms[0]) if terms else None

    items = [(qi, jk, stream) for qi in range(nq) for jk in [qi] + list(range(qi)) for stream in range(2)]
    ahead = [logits(*it) for it in items[:qk_ahead]]
    for n, (qi, jk, stream) in enumerate(items):
        s_cur, gate = ahead.pop(0)
        if gate is not None:
            select(qi, stream, gate)
        if n + qk_ahead < len(items):
            ahead.append(logits(*items[n + qk_ahead]))
        if jk == qi and stream == 0:
            prepare(qi)
        shift = row_shift(qi, jk, stream)
        c = jnp.max(s_cur, axis=0, keepdims=True)
        if shift is not None:
            c = c + shift
        if jk == qi:
            m_new = c
        else:
            m_prev = mx_ref[qi, stream]
            m_new = jnp.maximum(m_prev, c)
            alpha = jnp.exp2(m_prev - m_new)
        m_row = m_new if shift is None else m_new - shift
        pexp = jnp.exp2((s_cur - m_row).astype(_BF16))
        pv = _dot(vt_ref[jk, value_group(stream)], pexp)
        acc_ref[qi, stream] = pv if jk == qi else alpha * acc_ref[qi, stream] + pv
        mx_ref[qi, stream] = m_new
        if stream == 1 and jk == max(qi - 1, 0) and (qi == 0 or jk != qi):
            finish(qi)


def _diff_kernel(lam_init_ref, lamp_ref, q_ref, k_ref, v_ref, bias_ref, subg_ref, o_ref,
                 qlo_ref, qhi_ref, vt_ref, mx_ref, acc_ref):
    nq = vt_ref.shape[0]
    _stage_queries(q_ref, qlo_ref, qhi_ref)

    lf = lamp_ref[...]
    lam_init = lam_init_ref[0]
    lam = (jnp.exp(jnp.sum(lf[0:1, :] * lf[1:2, :], axis=-1, keepdims=True))
           - jnp.exp(jnp.sum(lf[2:3, :] * lf[3:4, :], axis=-1, keepdims=True)) + lam_init)

    def finish(qi):
        a0, a1 = acc_ref[qi, 0], acc_ref[qi, 1]
        out_t = (a0[:LANES] / a0[LANES:LANES + 1]
                 - lam * (a1[:LANES] / a1[LANES:LANES + 1]))
        ms = jnp.mean(out_t * out_t, axis=0, keepdims=True)
        out_t = out_t * lax.rsqrt(ms + NORM_EPS) * subg_ref[...] * (1.0 - lam_init)
        o_ref[0, qi] = out_t.astype(_BF16)

    _causal_attention(nq, qlo_ref, qhi_ref, k_ref, vt_ref, bias_ref, (0, 0), None, mx_ref, acc_ref,
                      lambda stream: 0, lambda qi: _stage_values(v_ref, vt_ref, qi), finish,
                      DIFF_QK_AHEAD)


def _diff_attention(qkv, bias_tiles, lam_init, lam_params, sub_g, batch, seq):
    nq = seq // ATT_TILE
    heads = BLK_PER_GROUP
    whole_seq = lambda col0: pl.BlockSpec((1, seq, LANES), lambda h, b: (col0 + h, b, 0))
    return pl.pallas_call(
        _diff_kernel,
        out_shape=jax.ShapeDtypeStruct((heads, batch * nq, LANES, ATT_TILE), _BF16),
        grid=(heads, batch),
        in_specs=[
            pl.BlockSpec(memory_space=pltpu.SMEM),
            pl.BlockSpec((4, HEAD_DIM), lambda h, b: (0, 0)),
            whole_seq(0),
            whole_seq(BLK_PER_GROUP),
            whole_seq(2 * BLK_PER_GROUP),
            pl.BlockSpec((1, nq, ATT_TILE, ATT_TILE), lambda h, b: (h, 0, 0, 0)),
            pl.BlockSpec((LANES, 1), lambda h, b: (0, 0)),
        ],
        out_specs=_transposed_out_spec(nq),
        scratch_shapes=[
            pltpu.VMEM((seq, LANES), _BF16),
            pltpu.VMEM((seq, LANES), _BF16),
            pltpu.VMEM((nq, 1, LANES + ONES_ROWS, ATT_TILE), _BF16),
            pltpu.VMEM((nq, 2, 1, ATT_TILE), _F32),
            pltpu.VMEM((nq, 2, LANES + ONES_ROWS, ATT_TILE), _F32),
        ],
        compiler_params=pltpu.CompilerParams(
            dimension_semantics=("arbitrary", "arbitrary"), vmem_limit_bytes=VMEM_LIMIT),
        name="diff_attn",
    )(lam_init, lam_params, qkv, qkv, qkv, bias_tiles, sub_g)


def _moba_kernel(q_ref, k_ref, v_ref, bias_ref, o_ref, qlo_ref, qhi_ref, vt_ref, mx_ref, acc_ref,
                 kmean_ref, sel_ref):
    t = ATT_TILE
    nq = vt_ref.shape[0]
    nb = k_ref.shape[1] // MOBA_BLOCK
    _stage_queries(q_ref, qlo_ref, qhi_ref)

    kmean_ref[...] = jnp.zeros(kmean_ref.shape, _F32)
    for n in range(nb):
        kb = k_ref[0, n * MOBA_BLOCK:(n + 1) * MOBA_BLOCK, :].astype(_F32)
        kmean_ref[n:n + 1, :] = jnp.sum(kb, axis=0, keepdims=True) * (1.0 / MOBA_BLOCK)

    kmean = kmean_ref[...]
    km_hi = kmean.astype(_BF16)
    km_lo = (kmean - km_hi.astype(_F32)).astype(_BF16)
    block = lax.broadcasted_iota(jnp.int32, (GATE_ROWS, t), 0)

    def select(qi, stream, gate_parts):
        gate = gate_parts[:GATE_ROWS] + gate_parts[GATE_ROWS:]
        for n in range(qi):
            g_n = gate[n:n + 1, :]
            beats = ((gate > g_n) | ((gate == g_n) & (block < n))) & (block < qi)
            rank = jnp.sum(jnp.where(beats, 1.0, 0.0), axis=0, keepdims=True)
            keep = (rank < MOBA_TOPK) & (jnp.abs(g_n) < jnp.inf)
            sel_ref[qi, stream, n:n + 1, :] = jnp.where(keep, 0.0, NEG)

    def finish(qi):
        heads_t = [acc_ref[qi, s, :HEAD_DIM] / acc_ref[qi, s, HEAD_DIM:HEAD_DIM + 1]
                   for s in range(2)]
        o_ref[0, qi] = jnp.concatenate(heads_t, axis=0).astype(_BF16)

    _causal_attention(nq, qlo_ref, qhi_ref, k_ref, vt_ref, bias_ref, (0, 1), sel_ref, mx_ref, acc_ref,
                      lambda stream: stream, lambda qi: _stage_values(v_ref, vt_ref, qi), finish,
                      MOBA_QK_AHEAD, gate_lhs=jnp.concatenate([km_hi, km_lo], axis=0), select=select)


def _moba_attention(qkv, bias_tiles, batch, seq):
    nq = seq // ATT_TILE
    pairs = BLK_PER_GROUP
    whole_seq = lambda col0: pl.BlockSpec((1, seq, LANES), lambda h, b: (col0 + h, b, 0))
    return pl.pallas_call(
        _moba_kernel,
        out_shape=jax.ShapeDtypeStruct((pairs, batch * nq, LANES, ATT_TILE), _BF16),
        grid=(pairs, batch),
        in_specs=[
            whole_seq(3 * BLK_PER_GROUP),
            whole_seq(4 * BLK_PER_GROUP),
            whole_seq(5 * BLK_PER_GROUP),
            pl.BlockSpec((2, nq, ATT_TILE, ATT_TILE), lambda h, b: (2 + h, 0, 0, 0)),
        ],
        out_specs=_transposed_out_spec(nq),
        scratch_shapes=[
            pltpu.VMEM((seq, LANES), _BF16),
            pltpu.VMEM((seq, LANES), _BF16),
            pltpu.VMEM((nq, 2, HEAD_DIM + ONES_ROWS, ATT_TILE), _BF16),
            pltpu.VMEM((nq, 2, 1, ATT_TILE), _F32),
            pltpu.VMEM((nq, 2, HEAD_DIM + ONES_ROWS, ATT_TILE), _F32),
            pltpu.VMEM((GATE_ROWS, LANES), _F32),
            pltpu.VMEM((nq, 2, GATE_ROWS, ATT_TILE), _F32),
        ],
        compiler_params=pltpu.CompilerParams(
            dimension_semantics=("arbitrary", "arbitrary"), vmem_limit_bytes=VMEM_LIMIT),
        name="moba_attn",
    )(qkv, qkv, qkv, bias_tiles)


def _ffn_kernel(tiles_per_seq, x_ref, yd_ref, ym_ref, wout_ref, g_ref, wup_ref, cw_ref, cb_ref,
                wdn_ref, o_ref, h_ref, act_ref, carry_ref):
    rows = x_ref.shape[0]
    heads = yd_ref.shape[0]

    @pl.when(pl.program_id(0) % tiles_per_seq == 0)
    def _():
        carry_ref[...] = jnp.zeros(carry_ref.shape, _F32)

    y_t = jnp.concatenate(
        [jnp.concatenate([y_ref[h, j] for j in range(y_ref.shape[1])], axis=-1)
         for y_ref in (yd_ref, ym_ref) for h in range(heads)], axis=0)
    x1 = x_ref[...] + lax.dot_general(y_t, wout_ref[...], (((0,), (0,)), ((), ())),
                                      preferred_element_type=_F32)
    o_ref[...] = x1
    ms = jnp.mean(x1 * x1, axis=-1, keepdims=True)
    h_ref[...] = (x1 * lax.rsqrt(ms + NORM_EPS) * g_ref[...]).astype(_BF16)

    first_rows = lax.broadcasted_iota(jnp.int32, (HALO, FFN_CHUNK), 0)

    def shifted(u, prev, k):
        rolled = pltpu.roll(u, k, 0)
        head = jnp.where(first_rows < k, pltpu.roll(prev, k, 0), rolled[:HALO])
        return jnp.concatenate([head, rolled[HALO:]], axis=0)

    def conv(u, col0):
        cols = slice(col0, col0 + FFN_CHUNK)
        prev = carry_ref[:, cols]
        carry_ref[:, cols] = u[rows - HALO:rows, :]
        w = cw_ref[:, cols]
        return (w[0:1, :] * shifted(u, prev, 2) + w[1:2, :] * shifted(u, prev, 1)
                + w[2:3, :] * u + cb_ref[:, cols])

    h = h_ref[...]
    n_chunks = D_FF // FFN_CHUNK
    head_cols = (n_chunks - 2) * FFN_CHUNK
    for c in range(n_chunks):
        col_g = c * FFN_CHUNK
        col_u = D_FF + c * FFN_CHUNK
        gate_raw = _dot(h, wup_ref[:, col_g:col_g + FFN_CHUNK])
        up_raw = _dot(h, wup_ref[:, col_u:col_u + FFN_CHUNK])
        if c == n_chunks - 1:
            o_ref[...] += _dot(act_ref[:, :head_cols], wdn_ref[:head_cols, :])
        gate = conv(gate_raw, col_g)
        up = conv(up_raw, col_u)
        act = gate / (1.0 + jnp.exp(-gate)) * up
        act_ref[:, col_g:col_g + FFN_CHUNK] = act.astype(_BF16)
    o_ref[...] += _dot(act_ref[:, head_cols:], wdn_ref[head_cols:, :])


def _ffn(x2d, yd, ym, w_out, g, w_up, conv_w, conv_b, w_down, layer, seq):
    n = x2d.shape[0]
    heads = yd.shape[0]
    resident = functools.partial(pl.BlockSpec, index_map=lambda r: (0, 0),
                                 pipeline_mode=pl.Buffered(1))
    layer_slab = lambda rows, cols: pl.BlockSpec((None, rows, cols), lambda r: (layer, 0, 0),
                                                 pipeline_mode=pl.Buffered(1))
    return pl.pallas_call(
        functools.partial(_ffn_kernel, seq // FFN_ROWS),
        out_shape=jax.ShapeDtypeStruct((n, D_MODEL), _F32),
        grid=(n // FFN_ROWS,),
        in_specs=[
            pl.BlockSpec((FFN_ROWS, D_MODEL), lambda r: (r, 0)),
            pl.BlockSpec((heads, FFN_ROWS // ATT_TILE, LANES, ATT_TILE), lambda r: (0, r, 0, 0)),
            pl.BlockSpec((heads, FFN_ROWS // ATT_TILE, LANES, ATT_TILE), lambda r: (0, r, 0, 0)),
            layer_slab(D_MODEL, D_MODEL),
            resident((1, D_MODEL)),
            layer_slab(D_MODEL, 2 * D_FF),
            resident((CONV_WIDTH, 2 * D_FF)),
            resident((1, 2 * D_FF)),
            layer_slab(D_FF, D_MODEL),
        ],
        out_specs=pl.BlockSpec((FFN_ROWS, D_MODEL), lambda r: (r, 0)),
        scratch_shapes=[
            pltpu.VMEM((FFN_ROWS, D_MODEL), _BF16),
            pltpu.VMEM((FFN_ROWS, D_FF), _BF16),
            pltpu.VMEM((HALO, 2 * D_FF), _F32),
        ],
        compiler_params=pltpu.CompilerParams(
            dimension_semantics=("arbitrary",), vmem_limit_bytes=VMEM_LIMIT),
        name="ffn",
    )(x2d, yd, ym, w_out, g, w_up, conv_w, conv_b, w_down)


def _distance_bucket(dist):
    n = jnp.maximum(dist, 0)
    nf = jnp.maximum(n, REL_MAX_EXACT).astype(_F32)
    large = REL_MAX_EXACT + (jnp.log(nf / REL_MAX_EXACT) / math.log(REL_MAX_DISTANCE / REL_MAX_EXACT)
                             * (REL_BUCKETS - REL_MAX_EXACT)).astype(jnp.int32)
    large = jnp.minimum(large, REL_BUCKETS - 1)
    return jnp.where(n < REL_MAX_EXACT, n, large)


def _bias_tile_kernel(w_ref, o_ref):
    t = ATT_TILE
    for d in range(o_ref.shape[1]):
        rows = jnp.broadcast_to(w_ref[0, d:d + 1, :], (t, 2 * t))
        o_ref[0, d] = pltpu.roll(rows, 0, 1, stride=1, stride_axis=0)[:, :t]


def _bias_tiles(rel_bias, seq):
    t = ATT_TILE
    nt = seq // t
    by_dist = rel_bias[_distance_bucket(jnp.arange(seq))].T * LOG2E
    heads = by_dist.shape[0]
    ext = jnp.concatenate([jnp.full((heads, t), NEG, _F32), by_dist], axis=1)
    idx = (jnp.arange(nt) * t)[:, None] + jnp.arange(t)[None, :]
    w = jnp.concatenate([ext[:, t + idx], ext[:, idx]], axis=-1)
    return pl.pallas_call(
        _bias_tile_kernel,
        out_shape=jax.ShapeDtypeStruct((heads, nt, t, t), _F32),
        grid=(heads,),
        in_specs=[pl.BlockSpec((1, nt, 2 * t), lambda h: (h, 0, 0))],
        out_specs=pl.BlockSpec((1, nt, t, t), lambda h: (h, 0, 0, 0)),
        compiler_params=pltpu.CompilerParams(dimension_semantics=("arbitrary",)),
        name="bias_tiles",
    )(w)


def kernel(x, ln_attn_g, w_in, qk_norm_g, diff_lambda, diff_subln_g, w_out, ln_ffn_g, w_up,
           conv_w, conv_b, w_down, rel_bias):
    batch, seq, d_model = x.shape
    depth = w_in.shape[0]
    assert d_model == D_MODEL and seq % ATT_TILE == 0 and ATT_TILE == MOBA_BLOCK
    assert seq % FFN_ROWS == 0 and FFN_ROWS % ATT_TILE == 0 and (batch * seq) % PROJ_ROWS == 0

    bias_tiles = _bias_tiles(rel_bias.astype(_F32), seq)
    w_in_b, w_out_b = w_in.astype(_BF16), w_out.astype(_BF16)
    w_up_b, w_down_b = w_up.astype(_BF16), w_down.astype(_BF16)
    qk_gain = jnp.concatenate([qk_norm_g, qk_norm_g], axis=-1).astype(_F32)

    xs = x.reshape(batch * seq, d_model).astype(_F32)
    for i in range(depth):
        lam_init = jnp.full((1,), 0.8 - 0.6 * math.exp(-0.3 * i), _F32)
        qkv = _proj(xs, ln_attn_g[i][None, :], w_in_b, i, qk_gain[i])
        yd = _diff_attention(qkv, bias_tiles, lam_init, diff_lambda[i].astype(_F32),
                             diff_subln_g[i][:, None], batch, seq)
        ym = _moba_attention(qkv, bias_tiles, batch, seq)
        xs = _ffn(xs, yd, ym, w_out_b, ln_ffn_g[i][None, :], w_up_b, conv_w[i],
                  conv_b[i][None, :], w_down_b, i, seq)
    return xs.reshape(batch, seq, d_model).astype(x.dtype)
```

```python
import functools
import math

import jax
import jax.numpy as jnp
from jax import lax
from jax.experimental import pallas as pl
from jax.experimental.pallas import tpu as pltpu

D_MODEL = 1024
HEAD_DIM = 64
LANES = 128
N_GROUPS = 6
GROUP_WIDTH = 512
N_COLBLK = N_GROUPS * GROUP_WIDTH // LANES
BLK_PER_GROUP = GROUP_WIDTH // LANES
D_FF = 2816
CONV_WIDTH = 3
MOBA_BLOCK = 256
MOBA_TOPK = 3
REL_BUCKETS = 32
REL_MAX_EXACT = 16
REL_MAX_DISTANCE = 1024
NORM_EPS = 1e-6
NEG = -1e30

ATT_TILE = 256
GATE_ROWS = 16
ONES_ROWS = 16
LOG2E = math.log2(math.e)
DIFF_QK_AHEAD = 5
MOBA_QK_AHEAD = 5
PROJ_ROWS = 1024
FFN_ROWS = 1024
FFN_CHUNK = 256
HALO = 8
VMEM_LIMIT = 56 * 1024 * 1024

_F32 = jnp.float32
_BF16 = jnp.bfloat16


def _dot(a, b):
    return jnp.dot(a, b, preferred_element_type=_F32)


def _dot_nt(a, b):
    return lax.dot_general(a, b, (((1,), (1,)), ((), ())), preferred_element_type=_F32)


def _proj_kernel(x_ref, g_ref, w_ref, qkg_ref, o_ref):
    x = x_ref[...]
    ms = jnp.mean(x * x, axis=-1, keepdims=True)
    h = (x * lax.rsqrt(ms + NORM_EPS) * g_ref[...]).astype(_BF16)
    rows = x.shape[0]
    low_half = lax.broadcasted_iota(jnp.int32, (rows, LANES), 1) < HEAD_DIM
    q_scale = HEAD_DIM ** -0.5 * LOG2E
    norm_groups = {0: (0, q_scale), 1: (1, 1.0), 3: (2, q_scale), 4: (3, 1.0)}
    for grp in range(N_GROUPS):
        yg = _dot(h, w_ref[:, grp * GROUP_WIDTH:(grp + 1) * GROUP_WIDTH])
        for c in range(BLK_PER_GROUP):
            y = yg[:, c * LANES:(c + 1) * LANES]
            if grp in norm_groups:
                gain_row, scale = norm_groups[grp]
                sq = y * y
                tot = jnp.sum(sq, axis=-1, keepdims=True)
                lo = jnp.sum(jnp.where(low_half, sq, 0.0), axis=-1, keepdims=True)
                ms_h = jnp.where(low_half, lo, tot - lo) * (1.0 / HEAD_DIM)
                gain = qkg_ref[gain_row:gain_row + 1, :]
                y = y * lax.rsqrt(ms_h + NORM_EPS) * gain
                if scale != 1.0:
                    y = y * scale
            o_ref[grp * BLK_PER_GROUP + c] = y.astype(_BF16)


def _proj(x2d, g, w_stack, layer, qkg):
    n = x2d.shape[0]
    return pl.pallas_call(
        _proj_kernel,
        out_shape=jax.ShapeDtypeStruct((N_COLBLK, n, LANES), _BF16),
        grid=(n // PROJ_ROWS,),
        in_specs=[
            pl.BlockSpec((PROJ_ROWS, D_MODEL), lambda r: (r, 0)),
            pl.BlockSpec((1, D_MODEL), lambda r: (0, 0)),
            pl.BlockSpec((None, D_MODEL, N_GROUPS * GROUP_WIDTH), lambda r: (layer, 0, 0),
                         pipeline_mode=pl.Buffered(1)),
            pl.BlockSpec((4, LANES), lambda r: (0, 0)),
        ],
        out_specs=pl.BlockSpec((N_COLBLK, PROJ_ROWS, LANES), lambda r: (0, r, 0)),
        compiler_params=pltpu.CompilerParams(
            dimension_semantics=("arbitrary",), vmem_limit_bytes=VMEM_LIMIT),
        name="proj",
    )(x2d, g, w_stack, qkg)


def _split_halves(q):
    low_half = lax.broadcasted_iota(jnp.int32, q.shape, 1) < HEAD_DIM
    zero = jnp.zeros_like(q)
    return jnp.where(low_half, q, zero), jnp.where(low_half, zero, q)


def _stage_queries(q_ref, qlo_ref, qhi_ref):
    q_lo, q_hi = _split_halves(q_ref[0])
    qlo_ref[...] = q_lo
    qhi_ref[...] = q_hi


def _stage_values(v_ref, vt_ref, n):
    feature_groups = vt_ref.shape[1]
    feat = LANES // feature_groups
    ones_rows = (lax.broadcasted_iota(jnp.int32, (ONES_ROWS, ATT_TILE), 0) == 0).astype(_BF16)
    v_t = v_ref[0, n * ATT_TILE:(n + 1) * ATT_TILE, :].astype(_F32).T.astype(_BF16)
    for g in range(feature_groups):
        vt_ref[n, g, 0:feat, :] = v_t[g * feat:(g + 1) * feat]
        vt_ref[n, g, feat:feat + ONES_ROWS, :] = ones_rows


def _bias_is_uniform(offset):
    return (offset - 1) * ATT_TILE + 1 >= REL_MAX_DISTANCE


def _transposed_out_spec(nq):
    return pl.BlockSpec((1, nq, LANES, ATT_TILE), lambda h, b: (h, b, 0, 0))


def _causal_attention(nq, qlo_ref, qhi_ref, k_ref, vt_ref, bias_ref, bias_rows, sel_ref, mx_ref, acc_ref,
                      value_group, prepare, finish, qk_ahead, gate_lhs=None, select=None):
    t = ATT_TILE

    def logits(qi, jk, stream):
        q_x = (qlo_ref, qhi_ref)[stream][qi * t:(qi + 1) * t, :]
        k_t = k_ref[0, jk * t:(jk + 1) * t, :]
        if _bias_is_uniform(qi - jk):
            return _dot_nt(k_t, q_x), None
        bias = bias_ref[bias_rows[stream], qi - jk]
        if jk == qi and qi > 0 and gate_lhs is not None:
            s_ext = _dot_nt(jnp.concatenate([k_t, gate_lhs], axis=0), q_x)
            return s_ext[:t] + bias, s_ext[t:]
        return _dot_nt(k_t, q_x) + bias, None

    def row_shift(qi, jk, stream):
        terms = []
        if jk != qi and sel_ref is not None:
            terms.append(sel_ref[qi, stream, jk:jk + 1, :])
        if _bias_is_uniform(qi - jk):
            terms.append(bias_ref[bias_rows[stream], qi - jk, 0:1, :])
        return sum(terms[1:], terms[0]) if terms else None

    items = [(qi, jk, stream) for qi in range(nq) for jk in [qi] + list(range(qi)) for stream in range(2)]
    ahead = [logits(*it) for it in items[:qk_ahead]]
    for n, (qi, jk, stream) in enumerate(items):
        s_cur, gate = ahead.pop(0)
        if gate is not None:
            select(qi, stream, gate)
        if n + qk_ahead < len(items):
            ahead.append(logits(*items[n + qk_ahead]))
        if jk == qi and stream == 0:
            prepare(qi)
        shift = row_shift(qi, jk, stream)
        c = jnp.max(s_cur, axis=0, keepdims=True)
        if shift is not None:
            c = c + shift
        if jk == qi:
            m_new = c
        else:
            m_prev = mx_ref[qi, stream]
            m_new = jnp.maximum(m_prev, c)
            alpha = jnp.exp2(m_prev - m_new)
        m_row = m_new if shift is None else m_new - shift
        pexp = jnp.exp2((s_cur - m_row).astype(_BF16))
        pv = _dot(vt_ref[jk, value_group(stream)], pexp)
        acc_ref[qi, stream] = pv if jk == qi else alpha * acc_ref[qi, stream] + pv
        mx_ref[qi, stream] = m_new
        if stream == 1 and jk == max(qi - 1, 0) and (qi == 0 or jk != qi):
            finish(qi)


def _diff_kernel(lam_init_ref, lamp_ref, q_ref, k_ref, v_ref, bias_ref, subg_ref, o_ref,
                 qlo_ref, qhi_ref, vt_ref, mx_ref, acc_ref):
    nq = vt_ref.shape[0]
    _stage_queries(q_ref, qlo_ref, qhi_ref)

    lf = lamp_ref[...]
    lam_init = lam_init_ref[0]
    lam = (jnp.exp(jnp.sum(lf[0:1, :] * lf[1:2, :], axis=-1, keepdims=True))
           - jnp.exp(jnp.sum(lf[2:3, :] * lf[3:4, :], axis=-1, keepdims=True)) + lam_init)

    def finish(qi):
        a0, a1 = acc_ref[qi, 0], acc_ref[qi, 1]
        out_t = (a0[:LANES] / a0[LANES:LANES + 1]
                 - lam * (a1[:LANES] / a1[LANES:LANES + 1]))
        ms = jnp.mean(out_t * out_t, axis=0, keepdims=True)
        out_t = out_t * lax.rsqrt(ms + NORM_EPS) * subg_ref[...] * (1.0 - lam_init)
        o_ref[0, qi] = out_t.astype(_BF16)

    _causal_attention(nq, qlo_ref, qhi_ref, k_ref, vt_ref, bias_ref, (0, 0), None, mx_ref, acc_ref,
                      lambda stream: 0, lambda qi: _stage_values(v_ref, vt_ref, qi), finish,
                      DIFF_QK_AHEAD)


def _diff_attention(qkv, bias_tiles, lam_init, lam_params, sub_g, batch, seq):
    nq = seq // ATT_TILE
    heads = BLK_PER_GROUP
    whole_seq = lambda col0: pl.BlockSpec((1, seq, LANES), lambda h, b: (col0 + h, b, 0))
    return pl.pallas_call(
        _diff_kernel,
        out_shape=jax.ShapeDtypeStruct((heads, batch * nq, LANES, ATT_TILE), _BF16),
        grid=(heads, batch),
        in_specs=[
            pl.BlockSpec(memory_space=pltpu.SMEM),
            pl.BlockSpec((4, HEAD_DIM), lambda h, b: (0, 0)),
            whole_seq(0),
            whole_seq(BLK_PER_GROUP),
            whole_seq(2 * BLK_PER_GROUP),
            pl.BlockSpec((1, nq, ATT_TILE, ATT_TILE), lambda h, b: (h, 0, 0, 0)),
            pl.BlockSpec((LANES, 1), lambda h, b: (0, 0)),
        ],
        out_specs=_transposed_out_spec(nq),
        scratch_shapes=[
            pltpu.VMEM((seq, LANES), _BF16),
            pltpu.VMEM((seq, LANES), _BF16),
            pltpu.VMEM((nq, 1, LANES + ONES_ROWS, ATT_TILE), _BF16),
            pltpu.VMEM((nq, 2, 1, ATT_TILE), _F32),
            pltpu.VMEM((nq, 2, LANES + ONES_ROWS, ATT_TILE), _F32),
        ],
        compiler_params=pltpu.CompilerParams(
            dimension_semantics=("arbitrary", "arbitrary"), vmem_limit_bytes=VMEM_LIMIT),
        name="diff_attn",
    )(lam_init, lam_params, qkv, qkv, qkv, bias_tiles, sub_g)


def _moba_kernel(q_ref, k_ref, v_ref, bias_ref, o_ref, qlo_ref, qhi_ref, vt_ref, mx_ref, acc_ref,
                 kmean_ref, sel_ref):
    t = ATT_TILE
    nq = vt_ref.shape[0]
    nb = k_ref.shape[1] // MOBA_BLOCK
    _stage_queries(q_ref, qlo_ref, qhi_ref)

    kmean_ref[...] = jnp.zeros(kmean_ref.shape, _F32)
    for n in range(nb):
        kb = k_ref[0, n * MOBA_BLOCK:(n + 1) * MOBA_BLOCK, :].astype(_F32)
        kmean_ref[n:n + 1, :] = jnp.sum(kb, axis=0, keepdims=True) * (1.0 / MOBA_BLOCK)

    kmean = kmean_ref[...]
    km_hi = kmean.astype(_BF16)
    km_lo = (kmean - km_hi.astype(_F32)).astype(_BF16)
    block = lax.broadcasted_iota(jnp.int32, (GATE_ROWS, t), 0)

    def select(qi, stream, gate_parts):
        gate = gate_parts[:GATE_ROWS] + gate_parts[GATE_ROWS:]
        for n in range(qi):
            g_n = gate[n:n + 1, :]
            beats = ((gate > g_n) | ((gate == g_n) & (block < n))) & (block < qi)
            rank = jnp.sum(jnp.where(beats, 1.0, 0.0), axis=0, keepdims=True)
            keep = (rank < MOBA_TOPK) & (jnp.abs(g_n) < jnp.inf)
            sel_ref[qi, stream, n:n + 1, :] = jnp.where(keep, 0.0, NEG)

    def finish(qi):
        heads_t = [acc_ref[qi, s, :HEAD_DIM] / acc_ref[qi, s, HEAD_DIM:HEAD_DIM + 1]
                   for s in range(2)]
        o_ref[0, qi] = jnp.concatenate(heads_t, axis=0).astype(_BF16)

    _causal_attention(nq, qlo_ref, qhi_ref, k_ref, vt_ref, bias_ref, (0, 1), sel_ref, mx_ref, acc_ref,
                      lambda stream: stream, lambda qi: _stage_values(v_ref, vt_ref, qi), finish,
                      MOBA_QK_AHEAD, gate_lhs=jnp.concatenate([km_hi, km_lo], axis=0), select=select)


def _moba_attention(qkv, bias_tiles, batch, seq):
    nq = seq // ATT_TILE
    pairs = BLK_PER_GROUP
    whole_seq = lambda col0: pl.BlockSpec((1, seq, LANES), lambda h, b: (col0 + h, b, 0))
    return pl.pallas_call(
        _moba_kernel,
        out_shape=jax.ShapeDtypeStruct((pairs, batch * nq, LANES, ATT_TILE), _BF16),
        grid=(pairs, batch),
        in_specs=[
            whole_seq(3 * BLK_PER_GROUP),
            whole_seq(4 * BLK_PER_GROUP),
            whole_seq(5 * BLK_PER_GROUP),
            pl.BlockSpec((2, nq, ATT_TILE, ATT_TILE), lambda h, b: (2 + h, 0, 0, 0)),
        ],
        out_specs=_transposed_out_spec(nq),
        scratch_shapes=[
            pltpu.VMEM((seq, LANES), _BF16),
            pltpu.VMEM((seq, LANES), _BF16),
            pltpu.VMEM((nq, 2, HEAD_DIM + ONES_ROWS, ATT_TILE), _BF16),
            pltpu.VMEM((nq, 2, 1, ATT_TILE), _F32),
            pltpu.VMEM((nq, 2, HEAD_DIM + ONES_ROWS, ATT_TILE), _F32),
            pltpu.VMEM((GATE_ROWS, LANES), _F32),
            pltpu.VMEM((nq, 2, GATE_ROWS, ATT_TILE), _F32),
        ],
        compiler_params=pltpu.CompilerParams(
            dimension_semantics=("arbitrary", "arbitrary"), vmem_limit_bytes=VMEM_LIMIT),
        name="moba_attn",
    )(qkv, qkv, qkv, bias_tiles)


def _ffn_kernel(tiles_per_seq, x_ref, yd_ref, ym_ref, wout_ref, g_ref, wup_ref, cw_ref, cb_ref,
                wdn_ref, o_ref, h_ref, act_ref, carry_ref):
    rows = x_ref.shape[0]
    heads = yd_ref.shape[0]

    @pl.when(pl.program_id(0) % tiles_per_seq == 0)
    def _():
        carry_ref[...] = jnp.zeros(carry_ref.shape, _F32)

    y_t = jnp.concatenate(
        [jnp.concatenate([y_ref[h, j] for j in range(y_ref.shape[1])], axis=-1)
         for y_ref in (yd_ref, ym_ref) for h in range(heads)], axis=0)
    x1 = x_ref[...] + lax.dot_general(y_t, wout_ref[...], (((0,), (0,)), ((), ())),
                                      preferred_element_type=_F32)
    o_ref[...] = x1
    ms = jnp.mean(x1 * x1, axis=-1, keepdims=True)
    h_ref[...] = (x1 * lax.rsqrt(ms + NORM_EPS) * g_ref[...]).astype(_BF16)

    first_rows = lax.broadcasted_iota(jnp.int32, (HALO, FFN_CHUNK), 0)

    def shifted(u, prev, k):
        rolled = pltpu.roll(u, k, 0)
        head = jnp.where(first_rows < k, pltpu.roll(prev, k, 0), rolled[:HALO])
        return jnp.concatenate([head, rolled[HALO:]], axis=0)

    def conv(u, col0):
        cols = slice(col0, col0 + FFN_CHUNK)
        prev = carry_ref[:, cols]
        carry_ref[:, cols] = u[rows - HALO:rows, :]
        w = cw_ref[:, cols]
        return (w[0:1, :] * shifted(u, prev, 2) + w[1:2, :] * shifted(u, prev, 1)
                + w[2:3, :] * u + cb_ref[:, cols])

    h = h_ref[...]
    n_chunks = D_FF // FFN_CHUNK
    head_cols = (n_chunks - 2) * FFN_CHUNK
    for c in range(n_chunks):
        col_g = c * FFN_CHUNK
        col_u = D_FF + c * FFN_CHUNK
        gate_raw = _dot(h, wup_ref[:, col_g:col_g + FFN_CHUNK])
        up_raw = _dot(h, wup_ref[:, col_u:col_u + FFN_CHUNK])
        if c == n_chunks - 1:
            o_ref[...] += _dot(act_ref[:, :head_cols], wdn_ref[:head_cols, :])
        gate = conv(gate_raw, col_g)
        up = conv(up_raw, col_u)
        half_gate = 0.5 * gate
        act = (half_gate + half_gate * jnp.tanh(half_gate)) * up
        act_ref[:, col_g:col_g + FFN_CHUNK] = act.astype(_BF16)
    o_ref[...] += _dot(act_ref[:, head_cols:], wdn_ref[head_cols:, :])


def _ffn(x2d, yd, ym, w_out, g, w_up, conv_w, conv_b, w_down, layer, seq):
    n = x2d.shape[0]
    heads = yd.shape[0]
    resident = functools.partial(pl.BlockSpec, index_map=lambda r: (0, 0),
                                 pipeline_mode=pl.Buffered(1))
    layer_slab = lambda rows, cols: pl.BlockSpec((None, rows, cols), lambda r: (layer, 0, 0),
                                                 pipeline_mode=pl.Buffered(1))
    return pl.pallas_call(
        functools.partial(_ffn_kernel, seq // FFN_ROWS),
        out_shape=jax.ShapeDtypeStruct((n, D_MODEL), _F32),
        grid=(n // FFN_ROWS,),
        in_specs=[
            pl.BlockSpec((FFN_ROWS, D_MODEL), lambda r: (r, 0)),
            pl.BlockSpec((heads, FFN_ROWS // ATT_TILE, LANES, ATT_TILE), lambda r: (0, r, 0, 0)),
            pl.BlockSpec((heads, FFN_ROWS // ATT_TILE, LANES, ATT_TILE), lambda r: (0, r, 0, 0)),
            layer_slab(D_MODEL, D_MODEL),
            resident((1, D_MODEL)),
            layer_slab(D_MODEL, 2 * D_FF),
            resident((CONV_WIDTH, 2 * D_FF)),
            resident((1, 2 * D_FF)),
            layer_slab(D_FF, D_MODEL),
        ],
        out_specs=pl.BlockSpec((FFN_ROWS, D_MODEL), lambda r: (r, 0)),
        scratch_shapes=[
            pltpu.VMEM((FFN_ROWS, D_MODEL), _BF16),
            pltpu.VMEM((FFN_ROWS, D_FF), _BF16),
            pltpu.VMEM((HALO, 2 * D_FF), _F32),
        ],
        compiler_params=pltpu.CompilerParams(
            dimension_semantics=("arbitrary",), vmem_limit_bytes=VMEM_LIMIT),
        name="ffn",
    )(x2d, yd, ym, w_out, g, w_up, conv_w, conv_b, w_down)


def _distance_bucket(dist):
    n = jnp.maximum(dist, 0)
    nf = jnp.maximum(n, REL_MAX_EXACT).astype(_F32)
    large = REL_MAX_EXACT + (jnp.log(nf / REL_MAX_EXACT) / math.log(REL_MAX_DISTANCE / REL_MAX_EXACT)
                             * (REL_BUCKETS - REL_MAX_EXACT)).astype(jnp.int32)
    large = jnp.minimum(large, REL_BUCKETS - 1)
    return jnp.where(n < REL_MAX_EXACT, n, large)


def _bias_tile_kernel(w_ref, o_ref):
    t = ATT_TILE
    for d in range(o_ref.shape[1]):
        rows = jnp.broadcast_to(w_ref[0, d:d + 1, :], (t, 2 * t))
        o_ref[0, d] = pltpu.roll(rows, 0, 1, stride=1, stride_axis=0)[:, :t]


def _bias_tiles(rel_bias, seq):
    t = ATT_TILE
    nt = seq // t
    by_dist = rel_bias[_distance_bucket(jnp.arange(seq))].T * LOG2E
    heads = by_dist.shape[0]
    ext = jnp.concatenate([jnp.full((heads, t), NEG, _F32), by_dist], axis=1)
    idx = (jnp.arange(nt) * t)[:, None] + jnp.arange(t)[None, :]
    w = jnp.concatenate([ext[:, t + idx], ext[:, idx]], axis=-1)
    return pl.pallas_call(
        _bias_tile_kernel,
        out_shape=jax.ShapeDtypeStruct((heads, nt, t, t), _F32),
        grid=(heads,),
        in_specs=[pl.BlockSpec((1, nt, 2 * t), lambda h: (h, 0, 0))],
        out_specs=pl.BlockSpec((1, nt, t, t), lambda h: (h, 0, 0, 0)),
        compiler_params=pltpu.CompilerParams(dimension_semantics=("arbitrary",)),
        name="bias_tiles",
    )(w)


def kernel(x, ln_attn_g, w_in, qk_norm_g, diff_lambda, diff_subln_g, w_out, ln_ffn_g, w_up,
           conv_w, conv_b, w_down, rel_bias):
    batch, seq, d_model = x.shape
    depth = w_in.shape[0]
    assert d_model == D_MODEL and seq % ATT_TILE == 0 and ATT_TILE == MOBA_BLOCK
    assert seq % FFN_ROWS == 0 and FFN_ROWS % ATT_TILE == 0 and (batch * seq) % PROJ_ROWS == 0

    bias_tiles = _bias_tiles(rel_bias.astype(_F32), seq)
    w_in_b, w_out_b = w_in.astype(_BF16), w_out.astype(_BF16)
    w_up_b, w_down_b = w_up.astype(_BF16), w_down.astype(_BF16)
    qk_gain = jnp.concatenate([qk_norm_g, qk_norm_g], axis=-1).astype(_F32)

    xs = x.reshape(batch * seq, d_model).astype(_F32)
    for i in range(depth):
        lam_init = jnp.full((1,), 0.8 - 0.6 * math.exp(-0.3 * i), _F32)
        qkv = _proj(xs, ln_attn_g[i][None, :], w_in_b, i, qk_gain[i])
        yd = _diff_attention(qkv, bias_tiles, lam_init, diff_lambda[i].astype(_F32),
                             diff_subln_g[i][:, None], batch, seq)
        ym = _moba_attention(qkv, bias_tiles, batch, seq)
        xs = _ffn(xs, yd, ym, w_out_b, ln_ffn_g[i][None, :], w_up_b, conv_w[i],
                  conv_b[i][None, :], w_down_b, i, seq)
    return xs.reshape(batch, seq, d_model).astype(x.dtype)
```

```python
import functools
import math

import jax
import jax.numpy as jnp
from jax import lax
from jax.experimental import pallas as pl
from jax.experimental.pallas import tpu as pltpu

D_MODEL = 1024
HEAD_DIM = 64
LANES = 128
N_GROUPS = 6
GROUP_WIDTH = 512
N_COLBLK = N_GROUPS * GROUP_WIDTH // LANES
BLK_PER_GROUP = GROUP_WIDTH // LANES
D_FF = 2816
CONV_WIDTH = 3
MOBA_BLOCK = 256
MOBA_TOPK = 3
REL_BUCKETS = 32
REL_MAX_EXACT = 16
REL_MAX_DISTANCE = 1024
NORM_EPS = 1e-6
NEG = -1e30

ATT_TILE = 256
GATE_ROWS = 16
ONES_ROWS = 16
LOG2E = math.log2(math.e)
DIFF_QK_AHEAD = 5
MOBA_QK_AHEAD = 5
PROJ_ROWS = 1024
FFN_ROWS = 1024
FFN_CHUNK = 256
HALO = 8
VMEM_LIMIT = 56 * 1024 * 1024

_F32 = jnp.float32
_BF16 = jnp.bfloat16


def _dot(a, b):
    return jnp.dot(a, b, preferred_element_type=_F32)


def _dot_nt(a, b):
    return lax.dot_general(a, b, (((1,), (1,)), ((), ())), preferred_element_type=_F32)


def _proj_kernel(x_ref, g_ref, w32_ref, qkg_ref, o_ref, w_ref):
    @pl.when(pl.program_id(0) == 0)
    def _():
        for grp in range(N_GROUPS):
            cols = slice(grp * GROUP_WIDTH, (grp + 1) * GROUP_WIDTH)
            w_ref[:, cols] = w32_ref[:, cols].astype(_BF16)

    x = x_ref[...]
    ms = jnp.mean(x * x, axis=-1, keepdims=True)
    h = (x * lax.rsqrt(ms + NORM_EPS) * g_ref[...]).astype(_BF16)
    rows = x.shape[0]
    low_half = lax.broadcasted_iota(jnp.int32, (rows, LANES), 1) < HEAD_DIM
    q_scale = HEAD_DIM ** -0.5 * LOG2E
    norm_groups = {0: (0, q_scale), 1: (1, 1.0), 3: (2, q_scale), 4: (3, 1.0)}
    for grp in range(N_GROUPS):
        yg = _dot(h, w_ref[:, grp * GROUP_WIDTH:(grp + 1) * GROUP_WIDTH])
        for c in range(BLK_PER_GROUP):
            y = yg[:, c * LANES:(c + 1) * LANES]
            if grp in norm_groups:
                gain_row, scale = norm_groups[grp]
                sq = y * y
                tot = jnp.sum(sq, axis=-1, keepdims=True)
                lo = jnp.sum(jnp.where(low_half, sq, 0.0), axis=-1, keepdims=True)
                ms_h = jnp.where(low_half, lo, tot - lo) * (1.0 / HEAD_DIM)
                gain = qkg_ref[gain_row:gain_row + 1, :]
                y = y * lax.rsqrt(ms_h + NORM_EPS) * gain
                if scale != 1.0:
                    y = y * scale
            o_ref[grp * BLK_PER_GROUP + c] = y.astype(_BF16)


def _proj(x2d, g, w_stack, layer, qkg):
    n = x2d.shape[0]
    return pl.pallas_call(
        _proj_kernel,
        out_shape=jax.ShapeDtypeStruct((N_COLBLK, n, LANES), _BF16),
        grid=(n // PROJ_ROWS,),
        in_specs=[
            pl.BlockSpec((PROJ_ROWS, D_MODEL), lambda r: (r, 0)),
            pl.BlockSpec((1, D_MODEL), lambda r: (0, 0)),
            pl.BlockSpec((None, D_MODEL, N_GROUPS * GROUP_WIDTH), lambda r: (layer, 0, 0),
                         pipeline_mode=pl.Buffered(1)),
            pl.BlockSpec((4, LANES), lambda r: (0, 0)),
        ],
        out_specs=pl.BlockSpec((N_COLBLK, PROJ_ROWS, LANES), lambda r: (0, r, 0)),
        scratch_shapes=[pltpu.VMEM((D_MODEL, N_GROUPS * GROUP_WIDTH), _BF16)],
        compiler_params=pltpu.CompilerParams(
            dimension_semantics=("arbitrary",), vmem_limit_bytes=VMEM_LIMIT),
        name="proj",
    )(x2d, g, w_stack, qkg)


def _split_halves(q):
    low_half = lax.broadcasted_iota(jnp.int32, q.shape, 1) < HEAD_DIM
    zero = jnp.zeros_like(q)
    return jnp.where(low_half, q, zero), jnp.where(low_half, zero, q)


def _stage_queries(q_ref, qlo_ref, qhi_ref):
    q_lo, q_hi = _split_halves(q_ref[0])
    qlo_ref[...] = q_lo
    qhi_ref[...] = q_hi


def _stage_values(v_ref, vt_ref, n):
    feature_groups = vt_ref.shape[1]
    feat = LANES // feature_groups
    ones_rows = (lax.broadcasted_iota(jnp.int32, (ONES_ROWS, ATT_TILE), 0) == 0).astype(_BF16)
    v_t = v_ref[0, n * ATT_TILE:(n + 1) * ATT_TILE, :].astype(_F32).T.astype(_BF16)
    for g in range(feature_groups):
        vt_ref[n, g, 0:feat, :] = v_t[g * feat:(g + 1) * feat]
        vt_ref[n, g, feat:feat + ONES_ROWS, :] = ones_rows


def _bias_is_uniform(offset):
    return (offset - 1) * ATT_TILE + 1 >= REL_MAX_DISTANCE


def _transposed_out_spec(nq):
    return pl.BlockSpec((1, nq, LANES, ATT_TILE), lambda h, b: (h, b, 0, 0))


def _causal_attention(nq, qlo_ref, qhi_ref, k_ref, vt_ref, bias_ref, bias_rows, sel_ref, mx_ref, acc_ref,
                      value_group, prepare, finish, qk_ahead, gate_lhs=None, select=None):
    t = ATT_TILE

    def logits(qi, jk, stream):
        q_x = (qlo_ref, qhi_ref)[stream][qi * t:(qi + 1) * t, :]
        k_t = k_ref[0, jk * t:(jk + 1) * t, :]
        if _bias_is_uniform(qi - jk):
            return _dot_nt(k_t, q_x), None
        bias = bias_ref[bias_rows[stream], qi - jk]
        if jk == qi and qi > 0 and gate_lhs is not None:
            s_ext = _dot_nt(jnp.concatenate([k_t, gate_lhs], axis=0), q_x)
            return s_ext[:t] + bias, s_ext[t:]
        return _dot_nt(k_t, q_x) + bias, None

    def row_shift(qi, jk, stream):
        terms = []
        if jk != qi and sel_ref is not None:
            terms.append(sel_ref[qi, stream, jk:jk + 1, :])
        if _bias_is_uniform(qi - jk):
            terms.append(bias_ref[bias_rows[stream], qi - jk, 0:1, :])
        return sum(terms[1:], terms[0]) if terms else None

    items = [(qi, jk, stream) for qi in range(nq) for jk in [qi] + list(range(qi)) for stream in range(2)]
    ahead = [logits(*it) for it in items[:qk_ahead]]
    for n, (qi, jk, stream) in enumerate(items):
        s_cur, gate = ahead.pop(0)
        if gate is not None:
            select(qi, stream, gate)
        if n + qk_ahead < len(items):
            ahead.append(logits(*items[n + qk_ahead]))
        if jk == qi and stream == 0:
            prepare(qi)
        shift = row_shift(qi, jk, stream)
        c = jnp.max(s_cur, axis=0, keepdims=True)
        if shift is not None:
            c = c + shift
        if jk == qi:
            m_new = c
        else:
            m_prev = mx_ref[qi, stream]
            m_new = jnp.maximum(m_prev, c)
            alpha = jnp.exp2(m_prev - m_new)
        m_row = m_new if shift is None else m_new - shift
        pexp = jnp.exp2((s_cur - m_row).astype(_BF16))
        pv = _dot(vt_ref[jk, value_group(stream)], pexp)
        acc_ref[qi, stream] = pv if jk == qi else alpha * acc_ref[qi, stream] + pv
        mx_ref[qi, stream] = m_new
        if stream == 1 and jk == max(qi - 1, 0) and (qi == 0 or jk != qi):
            finish(qi)


def _diff_kernel(lam_init_ref, lamp_ref, q_ref, k_ref, v_ref, bias_ref, subg_ref, o_ref,
                 qlo_ref, qhi_ref, vt_ref, mx_ref, acc_ref):
    nq = vt_ref.shape[0]
    _stage_queries(q_ref, qlo_ref, qhi_ref)

    lf = lamp_ref[...]
    lam_init = lam_init_ref[0]
    lam = (jnp.exp(jnp.sum(lf[0:1, :] * lf[1:2, :], axis=-1, keepdims=True))
           - jnp.exp(jnp.sum(lf[2:3, :] * lf[3:4, :], axis=-1, keepdims=True)) + lam_init)

    def finish(qi):
        a0, a1 = acc_ref[qi, 0], acc_ref[qi, 1]
        out_t = (a0[:LANES] / a0[LANES:LANES + 1]
                 - lam * (a1[:LANES] / a1[LANES:LANES + 1]))
        ms = jnp.mean(out_t * out_t, axis=0, keepdims=True)
        out_t = out_t * lax.rsqrt(ms + NORM_EPS) * subg_ref[...] * (1.0 - lam_init)
        o_ref[0, qi] = out_t.astype(_BF16)

    _causal_attention(nq, qlo_ref, qhi_ref, k_ref, vt_ref, bias_ref, (0, 0), None, mx_ref, acc_ref,
                      lambda stream: 0, lambda qi: _stage_values(v_ref, vt_ref, qi), finish,
                      DIFF_QK_AHEAD)


def _diff_attention(qkv, bias_tiles, lam_init, lam_params, sub_g, batch, seq):
    nq = seq // ATT_TILE
    heads = BLK_PER_GROUP
    whole_seq = lambda col0: pl.BlockSpec((1, seq, LANES), lambda h, b: (col0 + h, b, 0))
    return pl.pallas_call(
        _diff_kernel,
        out_shape=jax.ShapeDtypeStruct((heads, batch * nq, LANES, ATT_TILE), _BF16),
        grid=(heads, batch),
        in_specs=[
            pl.BlockSpec(memory_space=pltpu.SMEM),
            pl.BlockSpec((4, HEAD_DIM), lambda h, b: (0, 0)),
            whole_seq(0),
            whole_seq(BLK_PER_GROUP),
            whole_seq(2 * BLK_PER_GROUP),
            pl.BlockSpec((1, nq, ATT_TILE, ATT_TILE), lambda h, b: (h, 0, 0, 0)),
            pl.BlockSpec((LANES, 1), lambda h, b: (0, 0)),
        ],
        out_specs=_transposed_out_spec(nq),
        scratch_shapes=[
            pltpu.VMEM((seq, LANES), _BF16),
            pltpu.VMEM((seq, LANES), _BF16),
            pltpu.VMEM((nq, 1, LANES + ONES_ROWS, ATT_TILE), _BF16),
            pltpu.VMEM((nq, 2, 1, ATT_TILE), _F32),
            pltpu.VMEM((nq, 2, LANES + ONES_ROWS, ATT_TILE), _F32),
        ],
        compiler_params=pltpu.CompilerParams(
            dimension_semantics=("arbitrary", "arbitrary"), vmem_limit_bytes=VMEM_LIMIT),
        name="diff_attn",
    )(lam_init, lam_params, qkv, qkv, qkv, bias_tiles, sub_g)


def _moba_kernel(q_ref, k_ref, v_ref, bias_ref, o_ref, qlo_ref, qhi_ref, vt_ref, mx_ref, acc_ref,
                 kmean_ref, sel_ref):
    t = ATT_TILE
    nq = vt_ref.shape[0]
    nb = k_ref.shape[1] // MOBA_BLOCK
    _stage_queries(q_ref, qlo_ref, qhi_ref)

    kmean_ref[...] = jnp.zeros(kmean_ref.shape, _F32)
    for n in range(nb):
        kb = k_ref[0, n * MOBA_BLOCK:(n + 1) * MOBA_BLOCK, :].astype(_F32)
        kmean_ref[n:n + 1, :] = jnp.sum(kb, axis=0, keepdims=True) * (1.0 / MOBA_BLOCK)

    kmean = kmean_ref[...]
    km_hi = kmean.astype(_BF16)
    km_lo = (kmean - km_hi.astype(_F32)).astype(_BF16)
    block = lax.broadcasted_iota(jnp.int32, (GATE_ROWS, t), 0)

    def select(qi, stream, gate_parts):
        gate = gate_parts[:GATE_ROWS] + gate_parts[GATE_ROWS:]
        for n in range(qi):
            g_n = gate[n:n + 1, :]
            beats = ((gate > g_n) | ((gate == g_n) & (block < n))) & (block < qi)
            rank = jnp.sum(jnp.where(beats, 1.0, 0.0), axis=0, keepdims=True)
            keep = (rank < MOBA_TOPK) & (jnp.abs(g_n) < jnp.inf)
            sel_ref[qi, stream, n:n + 1, :] = jnp.where(keep, 0.0, NEG)

    def finish(qi):
        heads_t = [acc_ref[qi, s, :HEAD_DIM] / acc_ref[qi, s, HEAD_DIM:HEAD_DIM + 1]
                   for s in range(2)]
        o_ref[0, qi] = jnp.concatenate(heads_t, axis=0).astype(_BF16)

    _causal_attention(nq, qlo_ref, qhi_ref, k_ref, vt_ref, bias_ref, (0, 1), sel_ref, mx_ref, acc_ref,
                      lambda stream: stream, lambda qi: _stage_values(v_ref, vt_ref, qi), finish,
                      MOBA_QK_AHEAD, gate_lhs=jnp.concatenate([km_hi, km_lo], axis=0), select=select)


def _moba_attention(qkv, bias_tiles, batch, seq):
    nq = seq // ATT_TILE
    pairs = BLK_PER_GROUP
    whole_seq = lambda col0: pl.BlockSpec((1, seq, LANES), lambda h, b: (col0 + h, b, 0))
    return pl.pallas_call(
        _moba_kernel,
        out_shape=jax.ShapeDtypeStruct((pairs, batch * nq, LANES, ATT_TILE), _BF16),
        grid=(pairs, batch),
        in_specs=[
            whole_seq(3 * BLK_PER_GROUP),
            whole_seq(4 * BLK_PER_GROUP),
            whole_seq(5 * BLK_PER_GROUP),
            pl.BlockSpec((2, nq, ATT_TILE, ATT_TILE), lambda h, b: (2 + h, 0, 0, 0)),
        ],
        out_specs=_transposed_out_spec(nq),
        scratch_shapes=[
            pltpu.VMEM((seq, LANES), _BF16),
            pltpu.VMEM((seq, LANES), _BF16),
            pltpu.VMEM((nq, 2, HEAD_DIM + ONES_ROWS, ATT_TILE), _BF16),
            pltpu.VMEM((nq, 2, 1, ATT_TILE), _F32),
            pltpu.VMEM((nq, 2, HEAD_DIM + ONES_ROWS, ATT_TILE), _F32),
            pltpu.VMEM((GATE_ROWS, LANES), _F32),
            pltpu.VMEM((nq, 2, GATE_ROWS, ATT_TILE), _F32),
        ],
        compiler_params=pltpu.CompilerParams(
            dimension_semantics=("arbitrary", "arbitrary"), vmem_limit_bytes=VMEM_LIMIT),
        name="moba_attn",
    )(qkv, qkv, qkv, bias_tiles)


def _ffn_kernel(tiles_per_seq, x_ref, yd_ref, ym_ref, wout_ref, g_ref, wup_ref, cw_ref, cb_ref,
                wdn_ref, o_ref, h_ref, act_ref, carry_ref):
    rows = x_ref.shape[0]
    heads = yd_ref.shape[0]

    @pl.when(pl.program_id(0) % tiles_per_seq == 0)
    def _():
        carry_ref[...] = jnp.zeros(carry_ref.shape, _F32)

    y_t = jnp.concatenate(
        [jnp.concatenate([y_ref[h, j] for j in range(y_ref.shape[1])], axis=-1)
         for y_ref in (yd_ref, ym_ref) for h in range(heads)], axis=0)
    x1 = x_ref[...] + lax.dot_general(y_t, wout_ref[...], (((0,), (0,)), ((), ())),
                                      preferred_element_type=_F32)
    o_ref[...] = x1
    ms = jnp.mean(x1 * x1, axis=-1, keepdims=True)
    h_ref[...] = (x1 * lax.rsqrt(ms + NORM_EPS) * g_ref[...]).astype(_BF16)

    first_rows = lax.broadcasted_iota(jnp.int32, (HALO, FFN_CHUNK), 0)

    def shifted(u, prev, k):
        rolled = pltpu.roll(u, k, 0)
        head = jnp.where(first_rows < k, pltpu.roll(prev, k, 0), rolled[:HALO])
        return jnp.concatenate([head, rolled[HALO:]], axis=0)

    def conv(u, col0):
        cols = slice(col0, col0 + FFN_CHUNK)
        prev = carry_ref[:, cols]
        carry_ref[:, cols] = u[rows - HALO:rows, :]
        w = cw_ref[:, cols]
        return (w[0:1, :] * shifted(u, prev, 2) + w[1:2, :] * shifted(u, prev, 1)
                + w[2:3, :] * u + cb_ref[:, cols])

    h = h_ref[...]
    n_chunks = D_FF // FFN_CHUNK
    head_cols = (n_chunks - 2) * FFN_CHUNK
    for c in range(n_chunks):
        col_g = c * FFN_CHUNK
        col_u = D_FF + c * FFN_CHUNK
        gate_raw = _dot(h, wup_ref[:, col_g:col_g + FFN_CHUNK])
        up_raw = _dot(h, wup_ref[:, col_u:col_u + FFN_CHUNK])
        if c == n_chunks - 1:
            o_ref[...] += _dot(act_ref[:, :head_cols], wdn_ref[:head_cols, :])
        gate = conv(gate_raw, col_g)
        up = conv(up_raw, col_u)
        half_gate = 0.5 * gate
        act = (half_gate + half_gate * jnp.tanh(half_gate)) * up
        act_ref[:, col_g:col_g + FFN_CHUNK] = act.astype(_BF16)
    o_ref[...] += _dot(act_ref[:, head_cols:], wdn_ref[head_cols:, :])


def _ffn(x2d, yd, ym, w_out, g, w_up, conv_w, conv_b, w_down, layer, seq):
    n = x2d.shape[0]
    heads = yd.shape[0]
    resident = functools.partial(pl.BlockSpec, index_map=lambda r: (0, 0),
                                 pipeline_mode=pl.Buffered(1))
    layer_slab = lambda rows, cols: pl.BlockSpec((None, rows, cols), lambda r: (layer, 0, 0),
                                                 pipeline_mode=pl.Buffered(1))
    return pl.pallas_call(
        functools.partial(_ffn_kernel, seq // FFN_ROWS),
        out_shape=jax.ShapeDtypeStruct((n, D_MODEL), _F32),
        grid=(n // FFN_ROWS,),
        in_specs=[
            pl.BlockSpec((FFN_ROWS, D_MODEL), lambda r: (r, 0)),
            pl.BlockSpec((heads, FFN_ROWS // ATT_TILE, LANES, ATT_TILE), lambda r: (0, r, 0, 0)),
            pl.BlockSpec((heads, FFN_ROWS // ATT_TILE, LANES, ATT_TILE), lambda r: (0, r, 0, 0)),
            layer_slab(D_MODEL, D_MODEL),
            resident((1, D_MODEL)),
            layer_slab(D_MODEL, 2 * D_FF),
            resident((CONV_WIDTH, 2 * D_FF)),
            resident((1, 2 * D_FF)),
            layer_slab(D_FF, D_MODEL),
        ],
        out_specs=pl.BlockSpec((FFN_ROWS, D_MODEL), lambda r: (r, 0)),
        scratch_shapes=[
            pltpu.VMEM((FFN_ROWS, D_MODEL), _BF16),
            pltpu.VMEM((FFN_ROWS, D_FF), _BF16),
            pltpu.VMEM((HALO, 2 * D_FF), _F32),
        ],
        compiler_params=pltpu.CompilerParams(
            dimension_semantics=("arbitrary",), vmem_limit_bytes=VMEM_LIMIT),
        name="ffn",
    )(x2d, yd, ym, w_out, g, w_up, conv_w, conv_b, w_down)


def _distance_bucket(dist):
    n = jnp.maximum(dist, 0)
    nf = jnp.maximum(n, REL_MAX_EXACT).astype(_F32)
    large = REL_MAX_EXACT + (jnp.log(nf / REL_MAX_EXACT) / math.log(REL_MAX_DISTANCE / REL_MAX_EXACT)
                             * (REL_BUCKETS - REL_MAX_EXACT)).astype(jnp.int32)
    large = jnp.minimum(large, REL_BUCKETS - 1)
    return jnp.where(n < REL_MAX_EXACT, n, large)


def _bias_tile_kernel(w_ref, o_ref):
    t = ATT_TILE
    for d in range(o_ref.shape[1]):
        rows = jnp.broadcast_to(w_ref[0, d:d + 1, :], (t, 2 * t))
        o_ref[0, d] = pltpu.roll(rows, 0, 1, stride=1, stride_axis=0)[:, :t]


def _bias_tiles(rel_bias, seq):
    t = ATT_TILE
    nt = seq // t
    by_dist = rel_bias[_distance_bucket(jnp.arange(seq))].T * LOG2E
    heads = by_dist.shape[0]
    ext = jnp.concatenate([jnp.full((heads, t), NEG, _F32), by_dist], axis=1)
    idx = (jnp.arange(nt) * t)[:, None] + jnp.arange(t)[None, :]
    w = jnp.concatenate([ext[:, t + idx], ext[:, idx]], axis=-1)
    return pl.pallas_call(
        _bias_tile_kernel,
        out_shape=jax.ShapeDtypeStruct((heads, nt, t, t), _F32),
        grid=(heads,),
        in_specs=[pl.BlockSpec((1, nt, 2 * t), lambda h: (h, 0, 0))],
        out_specs=pl.BlockSpec((1, nt, t, t), lambda h: (h, 0, 0, 0)),
        compiler_params=pltpu.CompilerParams(dimension_semantics=("arbitrary",)),
        name="bias_tiles",
    )(w)


def kernel(x, ln_attn_g, w_in, qk_norm_g, diff_lambda, diff_subln_g, w_out, ln_ffn_g, w_up,
           conv_w, conv_b, w_down, rel_bias):
    batch, seq, d_model = x.shape
    depth = w_in.shape[0]
    assert d_model == D_MODEL and seq % ATT_TILE == 0 and ATT_TILE == MOBA_BLOCK
    assert seq % FFN_ROWS == 0 and FFN_ROWS % ATT_TILE == 0 and (batch * seq) % PROJ_ROWS == 0

    bias_tiles = _bias_tiles(rel_bias.astype(_F32), seq)
    w_out_b = w_out.astype(_BF16)
    w_up_b, w_down_b = w_up.astype(_BF16), w_down.astype(_BF16)
    qk_gain = jnp.concatenate([qk_norm_g, qk_norm_g], axis=-1).astype(_F32)

    xs = x.reshape(batch * seq, d_model).astype(_F32)
    for i in range(depth):
        lam_init = jnp.full((1,), 0.8 - 0.6 * math.exp(-0.3 * i), _F32)
        qkv = _proj(xs, ln_attn_g[i][None, :], w_in.astype(_F32), i, qk_gain[i])
        yd = _diff_attention(qkv, bias_tiles, lam_init, diff_lambda[i].astype(_F32),
                             diff_subln_g[i][:, None], batch, seq)
        ym = _moba_attention(qkv, bias_tiles, batch, seq)
        xs = _ffn(xs, yd, ym, w_out_b, ln_ffn_g[i][None, :], w_up_b, conv_w[i],
                  conv_b[i][None, :], w_down_b, i, seq)
    return xs.reshape(batch, seq, d_model).astype(x.dtype)
```
